```python
import jax, jax.numpy as jnp
from jax import lax
import numpy as np

D_MODEL = 2048
BATCH = 4
SEQ = 2048
DEPTH = 1

D_RNN = 2048
RNN_BLOCKS = 16
RNN_BLOCK = D_RNN // RNN_BLOCKS
RNN_CONV = 4
LRU_C = 8.0
D_CONV = 2048
CONF_WIDTH = 31
MEM_LEN = 256
XA_HEADS = 4
XA_HEAD_DIM = D_MODEL // XA_HEADS
N_KEYS = 128
N_EXPERTS = N_KEYS * N_KEYS
PEER_HEADS = 8
PEER_KEY_DIM = 128
PEER_HALF = PEER_KEY_DIM // 2
PEER_TOPK = 16
PEER_HK = PEER_HEADS * PEER_TOPK
PEER_TOKEN_BLOCK = 128
DN_ALPHA = (2 * DEPTH) ** 0.25
DN_BETA = (8 * DEPTH) ** -0.25
LN_EPS = 1e-5
IN_WIDTH = 2 * D_RNN + 2 * D_CONV + 2 * D_MODEL
IN_SPLITS = (D_RNN, 2 * D_RNN, 2 * D_RNN + D_CONV, 2 * D_RNN + 2 * D_CONV, 2 * D_RNN + 2 * D_CONV + D_MODEL)

kernel_name = 'hybrid_rglru_conformer_peer_deepnorm'


def _layer_norm(x, g, b):
    xf = x.astype(jnp.float32)
    mu = jnp.mean(xf, axis=-1, keepdims=True)
    var = jnp.mean(jnp.square(xf - mu), axis=-1, keepdims=True)
    y = (xf - mu) * lax.rsqrt(var + LN_EPS)
    return (y * g + b).astype(x.dtype)


def _causal_dwconv(x, w, b):
    k = w.shape[0]
    y = lax.conv_general_dilated(x, w[:, None, :].astype(x.dtype), window_strides=(1,), padding=[(k - 1, 0)],
                                 dimension_numbers=('NWC', 'WIO', 'NWC'), feature_group_count=x.shape[-1])
    return y + b


def _rg_lru(xr, w_a, b_a, w_x, b_x, lam):
    bsz, seq, _ = xr.shape
    xb = xr.reshape(bsz, seq, RNN_BLOCKS, RNN_BLOCK)
    r = jax.nn.sigmoid(jnp.einsum('bshi,hij->bshj', xb, w_a).reshape(bsz, seq, D_RNN) + b_a)
    i = jax.nn.sigmoid(jnp.einsum('bshi,hij->bshj', xb, w_x).reshape(bsz, seq, D_RNN) + b_x)
    log_a = (-LRU_C * r.astype(jnp.float32)) * jax.nn.softplus(-lam.astype(jnp.float32))
    a = jnp.exp(log_a)
    u = jnp.sqrt(-jnp.expm1(2.0 * log_a)) * (i * xr).astype(jnp.float32)

    def combine(left, right):
        a_l, b_l = left
        a_r, b_r = right
        return a_l * a_r, a_r * b_l + b_r

    _, h = lax.associative_scan(combine, (a, u), axis=1)
    return h.astype(xr.dtype)


def _hybrid_mixer(x, w_in, b_in, rnn_conv_w, rnn_conv_b, rnn_w_a, rnn_b_a, rnn_w_x, rnn_b_x, rnn_lambda,
                  w_rnn_out, conf_dw_w, conf_dw_b, conf_ln_g, conf_ln_b, w_conf_out, b_conf_out, w_mix_out):
    proj = x @ w_in + b_in
    xr, gr, ca, cg, g_rnn, g_conf = jnp.split(proj, IN_SPLITS, axis=-1)
    xr = _causal_dwconv(xr, rnn_conv_w, rnn_conv_b)
    h = _rg_lru(xr, rnn_w_a, rnn_b_a, rnn_w_x, rnn_b_x, rnn_lambda)
    y_rnn = (h * jax.nn.gelu(gr)) @ w_rnn_out
    c = ca * jax.nn.sigmoid(cg)
    c = _causal_dwconv(c, conf_dw_w, conf_dw_b)
    c = jax.nn.silu(_layer_norm(c, conf_ln_g, conf_ln_b))
    y_conf = c @ w_conf_out + b_conf_out
    merged = jax.nn.sigmoid(g_rnn) * y_rnn + jax.nn.sigmoid(g_conf) * y_conf
    return merged @ w_mix_out


def _memory_cross_attention(x, mem, w_q, w_k, w_v, w_o):
    bsz, seq, _ = x.shape
    m = mem.shape[1]
    q = (x @ w_q).reshape(bsz, seq, XA_HEADS, XA_HEAD_DIM)
    k = (mem @ w_k).reshape(bsz, m, XA_HEADS, XA_HEAD_DIM)
    v = (mem @ w_v).reshape(bsz, m, XA_HEADS, XA_HEAD_DIM)
    s = jnp.einsum('bshd,bmhd->bhsm', q, k).astype(jnp.float32) * (XA_HEAD_DIM ** -0.5)
    p = jax.nn.softmax(s, axis=-1).astype(x.dtype)
    o = jnp.einsum('bhsm,bmhd->bshd', p, v).reshape(bsz, seq, D_MODEL)
    return o @ w_o


def _peer(x, w_q, sub_keys, expert_u, expert_v):
    bsz, seq, d = x.shape
    t = bsz * seq
    xt = x.reshape(t, d)
    q = (xt @ w_q).reshape(t, PEER_HEADS, 2, PEER_HALF).astype(jnp.float32)
    s = jnp.einsum('thcd,ckd->thck', q, sub_keys.astype(jnp.float32))
    s_top, i_top = lax.top_k(s, PEER_TOPK)
    cand = (s_top[:, :, 0, :, None] + s_top[:, :, 1, None, :]).reshape(t, PEER_HEADS, PEER_TOPK * PEER_TOPK)
    best, pos = lax.top_k(cand, PEER_TOPK)
    e1 = jnp.take_along_axis(i_top[:, :, 0], pos // PEER_TOPK, axis=-1)
    e2 = jnp.take_along_axis(i_top[:, :, 1], pos % PEER_TOPK, axis=-1)
    experts = (e1 * N_KEYS + e2).reshape(t, PEER_HK)
    gates = jax.nn.softmax(best, axis=-1).reshape(t, PEER_HK).astype(x.dtype)
    nb = t // PEER_TOKEN_BLOCK

    def token_block(args):
        xb, eb, gb = args
        u = expert_u[eb]
        v = expert_v[eb]
        act = jax.nn.gelu(jnp.einsum('td,tkd->tk', xb, u)) * gb
        return jnp.einsum('tk,tkd->td', act, v)

    out = lax.map(token_block, (xt.reshape(nb, PEER_TOKEN_BLOCK, d),
                                experts.reshape(nb, PEER_TOKEN_BLOCK, PEER_HK),
                                gates.reshape(nb, PEER_TOKEN_BLOCK, PEER_HK)))
    return out.reshape(bsz, seq, d)


def setup_inputs(seed: int = 0) -> dict:
    key = jax.random.key(seed)
    keys = iter(jax.random.split(key, 40))
    f32 = jnp.float32

    def nrm(shape, scale):
        return jax.random.normal(next(keys), (DEPTH,) + shape, f32) * scale

    def gain(n):
        return 1.0 + nrm((n,), 0.01)

    lam_u = jax.random.uniform(next(keys), (DEPTH, D_RNN), f32, minval=0.9, maxval=0.999)
    lam_p = lam_u ** (1.0 / LRU_C)
    rnn_lambda = jnp.log(lam_p) - jnp.log1p(-lam_p)
    return {
        'x': jax.random.normal(next(keys), (BATCH, SEQ, D_MODEL), f32),
        'mem': jax.random.normal(next(keys), (BATCH, MEM_LEN, D_MODEL), f32),
        'w_in': nrm((D_MODEL, IN_WIDTH), D_MODEL ** -0.5),
        'b_in': nrm((IN_WIDTH,), 0.01),
        'rnn_conv_w': nrm((RNN_CONV, D_RNN), RNN_CONV ** -0.5),
        'rnn_conv_b': nrm((D_RNN,), 0.01),
        'rnn_w_a': nrm((RNN_BLOCKS, RNN_BLOCK, RNN_BLOCK), RNN_BLOCK ** -0.5),
        'rnn_b_a': nrm((D_RNN,), 0.01),
        'rnn_w_x': nrm((RNN_BLOCKS, RNN_BLOCK, RNN_BLOCK), RNN_BLOCK ** -0.5),
        'rnn_b_x': nrm((D_RNN,), 0.01),
        'rnn_lambda': rnn_lambda,
        'w_rnn_out': nrm((D_RNN, D_MODEL), D_RNN ** -0.5 * DN_BETA),
        'conf_dw_w': nrm((CONF_WIDTH, D_CONV), CONF_WIDTH ** -0.5),
        'conf_dw_b': nrm((D_CONV,), 0.01),
        'conf_ln_g': gain(D_CONV),
        'conf_ln_b': nrm((D_CONV,), 0.01),
        'w_conf_out': nrm((D_CONV, D_MODEL), D_CONV ** -0.5 * DN_BETA),
        'b_conf_out': nrm((D_MODEL,), 0.01),
        'w_mix_out': nrm((D_MODEL, D_MODEL), D_MODEL ** -0.5 * DN_BETA),
        'ln1_g': gain(D_MODEL),
        'ln1_b': nrm((D_MODEL,), 0.01),
        'xa_w_q': nrm((D_MODEL, D_MODEL), D_MODEL ** -0.5),
        'xa_w_k': nrm((D_MODEL, D_MODEL), D_MODEL ** -0.5),
        'xa_w_v': nrm((D_MODEL, D_MODEL), D_MODEL ** -0.5 * DN_BETA),
        'xa_w_o': nrm((D_MODEL, D_MODEL), D_MODEL ** -0.5 * DN_BETA),
        'ln2_g': gain(D_MODEL),
        'ln2_b': nrm((D_MODEL,), 0.01),
        'peer_w_q': nrm((D_MODEL, PEER_HEADS * PEER_KEY_DIM), D_MODEL ** -0.5),
        'peer_sub_keys': nrm((2, N_KEYS, PEER_HALF), PEER_HALF ** -0.5),
        'peer_u': nrm((N_EXPERTS, D_MODEL), D_MODEL ** -0.5),
        'peer_v': nrm((N_EXPERTS, D_MODEL), DN_BETA),
        'ln3_g': gain(D_MODEL),
        'ln3_b': nrm((D_MODEL,), 0.01),
    }


def reference(x, mem, w_in, b_in, rnn_conv_w, rnn_conv_b, rnn_w_a, rnn_b_a, rnn_w_x, rnn_b_x, rnn_lambda,
              w_rnn_out, conf_dw_w, conf_dw_b, conf_ln_g, conf_ln_b, w_conf_out, b_conf_out, w_mix_out,
              ln1_g, ln1_b, xa_w_q, xa_w_k, xa_w_v, xa_w_o, ln2_g, ln2_b,
              peer_w_q, peer_sub_keys, peer_u, peer_v, ln3_g, ln3_b):
    for l in range(DEPTH):
        mix = _hybrid_mixer(x, w_in[l], b_in[l], rnn_conv_w[l], rnn_conv_b[l], rnn_w_a[l], rnn_b_a[l],
                            rnn_w_x[l], rnn_b_x[l], rnn_lambda[l], w_rnn_out[l], conf_dw_w[l], conf_dw_b[l],
                            conf_ln_g[l], conf_ln_b[l], w_conf_out[l], b_conf_out[l], w_mix_out[l])
        x = _layer_norm(DN_ALPHA * x + mix, ln1_g[l], ln1_b[l])
        xa = _memory_cross_attention(x, mem, xa_w_q[l], xa_w_k[l], xa_w_v[l], xa_w_o[l])
        x = _layer_norm(DN_ALPHA * x + xa, ln2_g[l], ln2_b[l])
        ff = _peer(x, peer_w_q[l], peer_sub_keys[l], peer_u[l], peer_v[l])
        x = _layer_norm(DN_ALPHA * x + ff, ln3_g[l], ln3_b[l])
    return x
```

```python
import functools
import math

import jax
import jax.numpy as jnp
from jax import lax
from jax.experimental import pallas as pl
from jax.experimental.pallas import tpu as pltpu

F32 = jnp.float32
BF16 = jnp.bfloat16

LRU_C = 8.0
LN_EPS = 1e-5
XA_HEADS = 4
PEER_TOPK = 16
VMEM_LIMIT_BYTES = 56 * 1024 * 1024
LANES = 128
SUBLANES = 8


def _cparams(*sem):
    return pltpu.CompilerParams(dimension_semantics=sem, vmem_limit_bytes=VMEM_LIMIT_BYTES)


def _layer_norm_rows(z, g, b):
    mu = jnp.mean(z, axis=-1, keepdims=True)
    zc = z - mu
    var = jnp.mean(zc * zc, axis=-1, keepdims=True)
    return zc * lax.rsqrt(var + LN_EPS) * g + b


def _dot(a, b):
    return jnp.dot(a, b, preferred_element_type=F32)


def _proj_kernel(a_ref, w_ref, b_ref, o_ref, *, act):
    y = _dot(a_ref[...], w_ref[...]) + b_ref[...]
    if act == "gelu":
        y = jax.nn.gelu(y)
    elif act == "sigmoid":
        y = jax.nn.sigmoid(y)
    o_ref[...] = y.astype(o_ref.dtype)


def _proj(a, w, b, col0, ncols, act, out_dtype, tm, tn):
    m, k = a.shape
    c0 = col0 // tn
    return pl.pallas_call(
        functools.partial(_proj_kernel, act=act),
        grid=(ncols // tn, m // tm),
        in_specs=[
            pl.BlockSpec((tm, k), lambda j, i: (i, 0)),
            pl.BlockSpec((k, tn), lambda j, i: (0, j + c0)),
            pl.BlockSpec((1, tn), lambda j, i: (0, j + c0)),
        ],
        out_specs=pl.BlockSpec((tm, tn), lambda j, i: (i, j)),
        out_shape=jax.ShapeDtypeStruct((m, ncols), out_dtype),
        compiler_params=_cparams("parallel", "parallel"),
        name="proj_" + act,
    )(a, w, b)


def _glu_kernel(a_ref, w1_ref, w2_ref, b1_ref, b2_ref, o_ref):
    a = a_ref[...]
    y1 = _dot(a, w1_ref[...]) + b1_ref[...]
    y2 = _dot(a, w2_ref[...]) + b2_ref[...]
    o_ref[...] = (y1 * jax.nn.sigmoid(y2)).astype(o_ref.dtype)


def _glu(a, w, b, col1, col2, ncols, tm, tn):
    m, k = a.shape
    c1, c2 = col1 // tn, col2 // tn
    return pl.pallas_call(
        _glu_kernel,
        grid=(ncols // tn, m // tm),
        in_specs=[
            pl.BlockSpec((tm, k), lambda j, i: (i, 0)),
            pl.BlockSpec((k, tn), lambda j, i: (0, j + c1)),
            pl.BlockSpec((k, tn), lambda j, i: (0, j + c2)),
            pl.BlockSpec((1, tn), lambda j, i: (0, j + c1)),
            pl.BlockSpec((1, tn), lambda j, i: (0, j + c2)),
        ],
        out_specs=pl.BlockSpec((tm, tn), lambda j, i: (i, j)),
        out_shape=jax.ShapeDtypeStruct((m, ncols), F32),
        compiler_params=_cparams("parallel", "parallel"),
        name="proj_glu",
    )(a, w, w, b, b)


def _rnn_kernel(xr_ref, cw_ref, cb_ref, wa_ref, wx_ref, ba_ref, bx_ref, lam_ref, gg_ref, o_ref,
                ext_ref, a_ref, u_ref, h_ref):
    ts, ct = xr_ref.shape
    kw = cw_ref.shape[0]
    s_idx = pl.program_id(2)

    @pl.when(s_idx == 0)
    def _():
        ext_ref[0:SUBLANES, :] = jnp.zeros((SUBLANES, ct), F32)
        h_ref[...] = jnp.zeros_like(h_ref)

    ext_ref[SUBLANES:SUBLANES + ts, :] = xr_ref[...]
    y = jnp.zeros((ts, ct), F32) + cb_ref[...]
    for k in range(kw):
        y = y + cw_ref[k:k + 1, :] * ext_ref[pl.ds(SUBLANES - (kw - 1) + k, ts), :]
    ext_ref[0:SUBLANES, :] = ext_ref[ts:ts + SUBLANES, :]

    nblk = ct // LANES
    r_parts, i_parts = [], []
    for blk in range(nblk):
        yb = y[:, blk * LANES:(blk + 1) * LANES].astype(BF16)
        r_parts.append(_dot(yb, wa_ref[blk]))
        i_parts.append(_dot(yb, wx_ref[blk]))
    r = jax.nn.sigmoid(jnp.concatenate(r_parts, axis=1) + ba_ref[...])
    ig = jax.nn.sigmoid(jnp.concatenate(i_parts, axis=1) + bx_ref[...])
    lam = lam_ref[...]
    sp = jnp.maximum(-lam, 0.0) + jnp.log1p(jnp.exp(-jnp.abs(lam)))
    log_a = (-LRU_C * r) * sp
    a = jnp.exp(log_a)
    u = jnp.sqrt(1.0 - jnp.exp(2.0 * log_a)) * (ig * y)
    a_ref[...] = a
    u_ref[...] = u

    row = lax.broadcasted_iota(jnp.int32, (SUBLANES, ct), 0)

    def body(g, h):
        r0 = pl.multiple_of(g * SUBLANES, SUBLANES)
        av = a_ref[pl.ds(r0, SUBLANES), :]
        uv = u_ref[pl.ds(r0, SUBLANES), :]
        for d in (1, 2, 4):
            a_sh = jnp.where(row >= d, pltpu.roll(av, d, 0), 1.0)
            u_sh = jnp.where(row >= d, pltpu.roll(uv, d, 0), 0.0)
            uv = av * u_sh + uv
            av = av * a_sh
        hh = uv + av * h
        u_ref[pl.ds(r0, SUBLANES), :] = hh
        return hh[SUBLANES - 1:SUBLANES, :]

    h_last = lax.fori_loop(0, ts // SUBLANES, body, h_ref[...])
    h_ref[...] = h_last
    o_ref[...] = (u_ref[...] * gg_ref[...].astype(F32)).astype(o_ref.dtype)


def _rnn_branch(xr, cw, cb, wa, wx, ba, bx, lam, gg, ts, ct):
    bsz, seq, c = xr.shape
    kw = cw.shape[0]
    nb = ct // LANES
    vec = pl.BlockSpec((1, ct), lambda b, j, s: (0, j))
    return pl.pallas_call(
        _rnn_kernel,
        grid=(bsz, c // ct, seq // ts),
        in_specs=[
            pl.BlockSpec((None, ts, ct), lambda b, j, s: (b, s, j)),
            pl.BlockSpec((kw, ct), lambda b, j, s: (0, j)),
            vec,
            pl.BlockSpec((nb, LANES, LANES), lambda b, j, s: (j, 0, 0)),
            pl.BlockSpec((nb, LANES, LANES), lambda b, j, s: (j, 0, 0)),
            vec, vec, vec,
            pl.BlockSpec((None, ts, ct), lambda b, j, s: (b, s, j)),
        ],
        out_specs=pl.BlockSpec((None, ts, ct), lambda b, j, s: (b, s, j)),
        out_shape=jax.ShapeDtypeStruct((bsz, seq, c), BF16),
        scratch_shapes=[
            pltpu.VMEM((ts + SUBLANES, ct), F32),
            pltpu.VMEM((ts, ct), F32),
            pltpu.VMEM((ts, ct), F32),
            pltpu.VMEM((1, ct), F32),
        ],
        compiler_params=_cparams("parallel", "parallel", "arbitrary"),
        name="rnn_branch",
    )(xr, cw, cb, wa, wx, ba, bx, lam, gg)


CONV_HALO = 32
CONV_LANE_CHUNK = 512


def _conf_kernel(c_ref, dw_ref, db_ref, g_ref, b_ref, o_ref, ext_ref, y_ref):
    ts, ch = c_ref.shape
    kw = dw_ref.shape[0]
    s_idx = pl.program_id(1)

    @pl.when(s_idx == 0)
    def _():
        ext_ref[0:CONV_HALO, :] = jnp.zeros((CONV_HALO, ch), F32)

    ext_ref[CONV_HALO:CONV_HALO + ts, :] = c_ref[...]
    off0 = CONV_HALO - (kw - 1)
    win_rows = CONV_HALO + SUBLANES
    lc = min(CONV_LANE_CHUNK, ch)
    n_lane = ch // lc

    def body(it, carry):
        rb = it // n_lane
        lb = it % n_lane
        r0 = pl.multiple_of(rb * SUBLANES, SUBLANES)
        l0 = pl.multiple_of(lb * lc, lc)
        win = ext_ref[pl.ds(r0, win_rows), pl.ds(l0, lc)]
        w = dw_ref[:, pl.ds(l0, lc)]
        acc = jnp.zeros((SUBLANES, lc), F32) + db_ref[:, pl.ds(l0, lc)]
        for k in range(kw):
            acc = acc + w[k:k + 1, :] * win[off0 + k:off0 + k + SUBLANES, :]
        y_ref[pl.ds(r0, SUBLANES), pl.ds(l0, lc)] = acc
        return carry

    lax.fori_loop(0, (ts // SUBLANES) * n_lane, body, 0)
    ext_ref[0:CONV_HALO, :] = ext_ref[ts:ts + CONV_HALO, :]
    yn = _layer_norm_rows(y_ref[...], g_ref[...], b_ref[...])
    o_ref[...] = (yn * jax.nn.sigmoid(yn)).astype(o_ref.dtype)


def _conf_branch(c, dw, db, g, b, ts):
    bsz, seq, ch = c.shape
    kw = dw.shape[0]
    assert kw - 1 <= CONV_HALO
    vec = pl.BlockSpec((1, ch), lambda bi, s: (0, 0))
    return pl.pallas_call(
        _conf_kernel,
        grid=(bsz, seq // ts),
        in_specs=[
            pl.BlockSpec((None, ts, ch), lambda bi, s: (bi, s, 0)),
            pl.BlockSpec((kw, ch), lambda bi, s: (0, 0)),
            vec, vec, vec,
        ],
        out_specs=pl.BlockSpec((None, ts, ch), lambda bi, s: (bi, s, 0)),
        out_shape=jax.ShapeDtypeStruct((bsz, seq, ch), BF16),
        scratch_shapes=[
            pltpu.VMEM((ts + CONV_HALO, ch), F32),
            pltpu.VMEM((ts, ch), F32),
        ],
        compiler_params=_cparams("parallel", "arbitrary"),
        name="conf_branch",
    )(c, dw, db, g, b)


def _merge_kernel(hr_ref, c_ref, wr_ref, wc_ref, bc_ref, gr_ref, gc_ref, o_ref):
    yr = _dot(hr_ref[...], wr_ref[...])
    yc = _dot(c_ref[...], wc_ref[...]) + bc_ref[...]
    o_ref[...] = (gr_ref[...].astype(F32) * yr + gc_ref[...].astype(F32) * yc).astype(o_ref.dtype)


def _merge(hr, cact, wr, wc, bc, gates, d_model, tm, tn):
    m, k = hr.shape
    goff = d_model // tn
    return pl.pallas_call(
        _merge_kernel,
        grid=(d_model // tn, m // tm),
        in_specs=[
            pl.BlockSpec((tm, k), lambda j, i: (i, 0)),
            pl.BlockSpec((tm, k), lambda j, i: (i, 0)),
            pl.BlockSpec((k, tn), lambda j, i: (0, j)),
            pl.BlockSpec((k, tn), lambda j, i: (0, j)),
            pl.BlockSpec((1, tn), lambda j, i: (0, j)),
            pl.BlockSpec((tm, tn), lambda j, i: (i, j)),
            pl.BlockSpec((tm, tn), lambda j, i: (i, j + goff)),
        ],
        out_specs=pl.BlockSpec((tm, tn), lambda j, i: (i, j)),
        out_shape=jax.ShapeDtypeStruct((m, d_model), BF16),
        compiler_params=_cparams("parallel", "parallel"),
        name="merge",
    )(hr, cact, wr, wc, bc, gates, gates)


def _mm_ln_kernel(a_ref, w_ref, res_ref, g_ref, b_ref, o_ref, obf_ref, *, alpha):
    y = _dot(a_ref[...], w_ref[...])
    z = _layer_norm_rows(alpha * res_ref[...] + y, g_ref[...], b_ref[...])
    o_ref[...] = z
    obf_ref[...] = z.astype(BF16)


def _mm_ln(a, w, res, g, b, alpha, tm):
    m, k = a.shape
    n = w.shape[1]
    vec = pl.BlockSpec((1, n), lambda i: (0, 0))
    return pl.pallas_call(
        functools.partial(_mm_ln_kernel, alpha=alpha),
        grid=(m // tm,),
        in_specs=[
            pl.BlockSpec((tm, k), lambda i: (i, 0)),
            pl.BlockSpec((k, n), lambda i: (0, 0)),
            pl.BlockSpec((tm, n), lambda i: (i, 0)),
            vec, vec,
        ],
        out_specs=[pl.BlockSpec((tm, n), lambda i: (i, 0)), pl.BlockSpec((tm, n), lambda i: (i, 0))],
        out_shape=[jax.ShapeDtypeStruct((m, n), F32), jax.ShapeDtypeStruct((m, n), BF16)],
        compiler_params=_cparams("parallel"),
        name="mm_ln",
    )(a, w, res, g, b)


def _add_ln_kernel(y_ref, res_ref, g_ref, b_ref, o_ref, *, alpha):
    o_ref[...] = _layer_norm_rows(alpha * res_ref[...] + y_ref[...], g_ref[...], b_ref[...])


def _add_ln(y, res, g, b, alpha, tm):
    m, n = y.shape
    vec = pl.BlockSpec((1, n), lambda i: (0, 0))
    return pl.pallas_call(
        functools.partial(_add_ln_kernel, alpha=alpha),
        grid=(m // tm,),
        in_specs=[pl.BlockSpec((tm, n), lambda i: (i, 0)), pl.BlockSpec((tm, n), lambda i: (i, 0)), vec, vec],
        out_specs=pl.BlockSpec((tm, n), lambda i: (i, 0)),
        out_shape=jax.ShapeDtypeStruct((m, n), F32),
        compiler_params=_cparams("parallel"),
        name="add_ln",
    )(y, res, g, b)


def _attn_kernel(q_ref, k_ref, v_ref, o_ref, *, heads):
    tq, d = q_ref.shape
    hd = d // heads
    scale = hd ** -0.5
    outs = []
    for h in range(heads):
        sl = slice(h * hd, (h + 1) * hd)
        s = lax.dot_general(q_ref[:, sl], k_ref[:, sl], (((1,), (1,)), ((), ())),
                            preferred_element_type=F32) * scale
        m = jnp.max(s, axis=-1, keepdims=True)
        p = jnp.exp(s - m)
        p = p / jnp.sum(p, axis=-1, keepdims=True)
        outs.append(_dot(p.astype(BF16), v_ref[:, sl]))
    o_ref[...] = jnp.concatenate(outs, axis=1).astype(o_ref.dtype)


def _attention(q, kv, bsz, seq, mem_len, d_model, tq):
    nq = seq // tq
    return pl.pallas_call(
        functools.partial(_attn_kernel, heads=XA_HEADS),
        grid=(bsz, nq),
        in_specs=[
            pl.BlockSpec((tq, d_model), lambda b, i: (b * nq + i, 0)),
            pl.BlockSpec((mem_len, d_model), lambda b, i: (b, 0)),
            pl.BlockSpec((mem_len, d_model), lambda b, i: (b, 1)),
        ],
        out_specs=pl.BlockSpec((tq, d_model), lambda b, i: (b * nq + i, 0)),
        out_shape=jax.ShapeDtypeStruct((bsz * seq, d_model), BF16),
        compiler_params=_cparams("parallel", "parallel"),
        name="xattn",
    )(q, kv, kv)


def _extract_topk(s, k):
    rows = s.shape[0]
    iota = lax.broadcasted_iota(jnp.int32, s.shape, 0)
    rank = jnp.full(s.shape, float(k), F32)
    vals = []
    for r in range(k):
        m = jnp.max(s, axis=0, keepdims=True)
        idx = jnp.min(jnp.where(s == m, iota, rows), axis=0, keepdims=True)
        hit = iota == idx
        rank = jnp.where(hit, float(r), rank)
        s = jnp.where(hit, -jnp.inf, s)
        vals.append(m)
    return vals, rank


def _peer_route_kernel(x_ref, wq_ref, keys_ref, c1_ref, e1_ref, r2_ref, e2_ref, *, heads, topk):
    n_keys, half = keys_ref.shape[1], keys_ref.shape[2]
    q = _dot(x_ref[...], wq_ref[...])
    nt = (((1,), (1,)), ((), ()))
    for h in range(heads):
        base = h * 2 * half
        s1 = lax.dot_general(keys_ref[0], q[:, base:base + half], nt, preferred_element_type=F32)
        s2 = lax.dot_general(keys_ref[1], q[:, base + half:base + 2 * half], nt, preferred_element_type=F32)
        v1, rank1 = _extract_topk(s1, topk)
        v2, rank2 = _extract_topk(s2, topk)
        v2_all = jnp.concatenate(v2, axis=0)
        cand = jnp.concatenate([v1[a] + v2_all for a in range(topk)], axis=0)
        _, rankc = _extract_topk(cand, topk)
        sel = rankc < float(topk)
        e1_top = [jnp.exp(v1[a] - v1[0]) for a in range(topk)]
        e2_top = jnp.exp(v2_all - v2[0])
        prod = jnp.concatenate([e1_top[a] * e2_top for a in range(topk)], axis=0)
        z = jnp.sum(jnp.where(sel, prod, 0.0), axis=0, keepdims=True)
        c1 = jnp.zeros_like(s1)
        for a in range(topk):
            cnt = jnp.sum(sel[a * topk:(a + 1) * topk].astype(F32), axis=0, keepdims=True)
            c1 = jnp.where(rank1 == float(a), cnt, c1)
        c1_ref[h] = c1
        e1_ref[h] = jnp.exp(s1 - v1[0])
        r2_ref[h] = rank2
        e2_ref[h] = jnp.exp(s2 - v2[0]) / z


def _peer_route(x_bf, wq, keys, heads, tt):
    t, d = x_bf.shape
    n_keys = keys.shape[1]
    out = jax.ShapeDtypeStruct((heads, n_keys, t), F32)
    ospec = pl.BlockSpec((heads, n_keys, tt), lambda i: (0, 0, i))
    return pl.pallas_call(
        functools.partial(_peer_route_kernel, heads=heads, topk=PEER_TOPK),
        grid=(t // tt,),
        in_specs=[
            pl.BlockSpec((tt, d), lambda i: (i, 0)),
            pl.BlockSpec(wq.shape, lambda i: (0, 0)),
            pl.BlockSpec(keys.shape, lambda i: (0, 0, 0)),
        ],
        out_specs=[ospec, ospec, ospec, ospec],
        out_shape=[out, out, out, out],
        compiler_params=_cparams("parallel"),
        name="peer_route",
    )(x_bf, wq, keys)


def _peer_dense_kernel(x_ref, u_ref, vt_ref, c1_ref, e1_ref, r2_ref, e2_ref, o_ref, acc_ref, act_ref, *, heads):
    e_idx = pl.program_id(1)
    ec = u_ref.shape[0]
    n_keys = r2_ref.shape[1]
    rows_per_step = ec // n_keys

    @pl.when(e_idx == 0)
    def _():
        acc_ref[...] = jnp.zeros_like(acc_ref)

    ht = lax.dot_general(u_ref[...], x_ref[...], (((1,), (1,)), ((), ())), preferred_element_type=F32)
    for ii in range(rows_per_step):
        i = e_idx * rows_per_step + ii
        g = jnp.zeros((n_keys, ht.shape[1]), F32)
        for h in range(heads):
            c_row = c1_ref[h, pl.ds(i, 1), :]
            e_row = e1_ref[h, pl.ds(i, 1), :]
            g = g + jnp.where(r2_ref[h] < c_row, e2_ref[h], 0.0) * e_row
        hb = ht[ii * n_keys:(ii + 1) * n_keys, :]
        act_ref[ii * n_keys:(ii + 1) * n_keys, :] = (jax.nn.gelu(hb) * g).astype(BF16)
    acc_ref[...] += _dot(vt_ref[...], act_ref[...])

    @pl.when(e_idx == pl.num_programs(1) - 1)
    def _():
        o_ref[...] = acc_ref[...].T


def _peer_dense(x_bf, u_bf, vt_bf, c1, e1, r2, e2, tt, ec):
    t, d = x_bf.shape
    n_exp = u_bf.shape[0]
    heads, n_keys, _ = c1.shape
    rspec = pl.BlockSpec((heads, n_keys, tt), lambda i, e: (0, 0, i))
    return pl.pallas_call(
        functools.partial(_peer_dense_kernel, heads=heads),
        grid=(t // tt, n_exp // ec),
        in_specs=[
            pl.BlockSpec((tt, d), lambda i, e: (i, 0)),
            pl.BlockSpec((ec, d), lambda i, e: (e, 0)),
            pl.BlockSpec((d, ec), lambda i, e: (0, e)),
            rspec, rspec, rspec, rspec,
        ],
        out_specs=pl.BlockSpec((tt, d), lambda i, e: (i, 0)),
        out_shape=jax.ShapeDtypeStruct((t, d), F32),
        scratch_shapes=[pltpu.VMEM((d, tt), F32), pltpu.VMEM((ec, tt), BF16)],
        compiler_params=_cparams("parallel", "arbitrary"),
        name="peer_dense",
    )(x_bf, u_bf, vt_bf, c1, e1, r2, e2)


def _transpose_cast_kernel(v_ref, o_ref):
    o_ref[...] = v_ref[...].T.astype(o_ref.dtype)


def _transpose_cast(v, te):
    n_exp, d = v.shape
    return pl.pallas_call(
        _transpose_cast_kernel,
        grid=(n_exp // te,),
        in_specs=[pl.BlockSpec((te, d), lambda e: (e, 0))],
        out_specs=pl.BlockSpec((d, te), lambda e: (0, e)),
        out_shape=jax.ShapeDtypeStruct((d, n_exp), BF16),
        compiler_params=_cparams("parallel"),
        name="transpose_cast",
    )(v)


def _tile(n, pref):
    t = min(n, pref)
    assert n % t == 0
    return t


def _layer(x, mem, w_in, b_in, rnn_conv_w, rnn_conv_b, rnn_w_a, rnn_b_a, rnn_w_x, rnn_b_x, rnn_lambda,
           w_rnn_out, conf_dw_w, conf_dw_b, conf_ln_g, conf_ln_b, w_conf_out, b_conf_out, w_mix_out,
           ln1_g, ln1_b, xa_w_q, xa_w_k, xa_w_v, xa_w_o, ln2_g, ln2_b,
           peer_w_q, peer_sub_keys, peer_u, peer_v, ln3_g, ln3_b, alpha):
    bsz, seq, d = x.shape
    t = bsz * seq
    mem_len = mem.shape[1]
    d_rnn = rnn_conv_w.shape[1]
    d_conv = conf_dw_w.shape[1]
    row = lambda v: v.reshape(1, -1)

    tm = _tile(t, 1024)
    tn = _tile(d, 1024)
    x2d = x.reshape(t, d)
    x_bf = x2d.astype(BF16)
    w_in_bf = w_in.astype(BF16)
    b_in2 = row(b_in)

    xr = _proj(x_bf, w_in_bf, b_in2, 0, d_rnn, "none", F32, tm, tn)
    gg = _proj(x_bf, w_in_bf, b_in2, d_rnn, d_rnn, "gelu", BF16, tm, tn)
    cglu = _glu(x_bf, w_in_bf, b_in2, 2 * d_rnn, 2 * d_rnn + d_conv, d_conv, tm, tn)
    mgates = _proj(x_bf, w_in_bf, b_in2, 2 * d_rnn + 2 * d_conv, 2 * d, "sigmoid", BF16, tm, tn)

    hr = _rnn_branch(xr.reshape(bsz, seq, d_rnn), rnn_conv_w, row(rnn_conv_b),
                     rnn_w_a.astype(BF16), rnn_w_x.astype(BF16), row(rnn_b_a), row(rnn_b_x),
                     row(rnn_lambda), gg.reshape(bsz, seq, d_rnn),
                     _tile(seq, 512), _tile(d_rnn, 512))
    cact = _conf_branch(cglu.reshape(bsz, seq, d_conv), conf_dw_w, row(conf_dw_b),
                        row(conf_ln_g), row(conf_ln_b), _tile(seq, 256))

    merged = _merge(hr.reshape(t, d_rnn), cact.reshape(t, d_conv), w_rnn_out.astype(BF16),
                    w_conf_out.astype(BF16), row(b_conf_out), mgates, d, tm, tn)
    tl = _tile(t, 512)
    x1, x1_bf = _mm_ln(merged, w_mix_out.astype(BF16), x2d, row(ln1_g), row(ln1_b), alpha, tl)

    zero_d = jnp.zeros((1, d), F32)
    q = _proj(x1_bf, xa_w_q.astype(BF16), zero_d, 0, d, "none", BF16, tm, tn)
    w_kv = jnp.concatenate([xa_w_k, xa_w_v], axis=1).astype(BF16)
    mem_bf = mem.reshape(bsz * mem_len, d).astype(BF16)
    kv = _proj(mem_bf, w_kv, jnp.zeros((1, 2 * d), F32), 0, 2 * d, "none", BF16,
               _tile(bsz * mem_len, 1024), tn)
    o = _attention(q, kv, bsz, seq, mem_len, d, _tile(seq, 512))
    x2, x2_bf = _mm_ln(o, xa_w_o.astype(BF16), x1, row(ln2_g), row(ln2_b), alpha, tl)

    heads = peer_w_q.shape[1] // (2 * peer_sub_keys.shape[2])
    c1, e1, r2, e2 = _peer_route(x2_bf, peer_w_q.astype(BF16), peer_sub_keys, heads, _tile(t, 256))
    n_exp = peer_u.shape[0]
    ec = _tile(n_exp, 512)
    vt_bf = _transpose_cast(peer_v, ec)
    ff = _peer_dense(x2_bf, peer_u.astype(BF16), vt_bf, c1, e1, r2, e2, _tile(t, 512), ec)
    x3 = _add_ln(ff, x2, row(ln3_g), row(ln3_b), alpha, tl)
    return x3.reshape(bsz, seq, d)


def kernel(x, mem, w_in, b_in, rnn_conv_w, rnn_conv_b, rnn_w_a, rnn_b_a, rnn_w_x, rnn_b_x, rnn_lambda, w_rnn_out, conf_dw_w, conf_dw_b, conf_ln_g, conf_ln_b, w_conf_out, b_conf_out, w_mix_out, ln1_g, ln1_b, xa_w_q, xa_w_k, xa_w_v, xa_w_o, ln2_g, ln2_b, peer_w_q, peer_sub_keys, peer_u, peer_v, ln3_g, ln3_b):
    depth = w_in.shape[0]
    alpha = (2 * depth) ** 0.25
    params = (w_in, b_in, rnn_conv_w, rnn_conv_b, rnn_w_a, rnn_b_a, rnn_w_x, rnn_b_x, rnn_lambda, w_rnn_out,
              conf_dw_w, conf_dw_b, conf_ln_g, conf_ln_b, w_conf_out, b_conf_out, w_mix_out, ln1_g, ln1_b,
              xa_w_q, xa_w_k, xa_w_v, xa_w_o, ln2_g, ln2_b, peer_w_q, peer_sub_keys, peer_u, peer_v,
              ln3_g, ln3_b)
    for l in range(depth):
        x = _layer(x, mem, *[p[l] for p in params], alpha)
    return x
```

```python
import functools
import math

import jax
import jax.numpy as jnp
from jax import lax
from jax.experimental import pallas as pl
from jax.experimental.pallas import tpu as pltpu

F32 = jnp.float32
BF16 = jnp.bfloat16

LRU_C = 8.0
LN_EPS = 1e-5
XA_HEADS = 4
PEER_TOPK = 16
VMEM_LIMIT_BYTES = 56 * 1024 * 1024
LANES = 128
SUBLANES = 8


def _cparams(*sem):
    return pltpu.CompilerParams(dimension_semantics=sem, vmem_limit_bytes=VMEM_LIMIT_BYTES)


def _layer_norm_rows(z, g, b):
    mu = jnp.mean(z, axis=-1, keepdims=True)
    zc = z - mu
    var = jnp.mean(zc * zc, axis=-1, keepdims=True)
    return zc * lax.rsqrt(var + LN_EPS) * g + b


def _dot(a, b):
    return jnp.dot(a, b, preferred_element_type=F32)


def _proj_kernel(a_ref, w_ref, b_ref, o_ref, *, act):
    y = _dot(a_ref[...], w_ref[...]) + b_ref[...]
    if act == "gelu":
        y = jax.nn.gelu(y)
    elif act == "sigmoid":
        y = jax.nn.sigmoid(y)
    o_ref[...] = y.astype(o_ref.dtype)


def _proj(a, w, b, col0, ncols, act, out_dtype, tm, tn):
    m, k = a.shape
    c0 = col0 // tn
    return pl.pallas_call(
        functools.partial(_proj_kernel, act=act),
        grid=(ncols // tn, m // tm),
        in_specs=[
            pl.BlockSpec((tm, k), lambda j, i: (i, 0)),
            pl.BlockSpec((k, tn), lambda j, i: (0, j + c0)),
            pl.BlockSpec((1, tn), lambda j, i: (0, j + c0)),
        ],
        out_specs=pl.BlockSpec((tm, tn), lambda j, i: (i, j)),
        out_shape=jax.ShapeDtypeStruct((m, ncols), out_dtype),
        compiler_params=_cparams("parallel", "parallel"),
        name="proj_" + act,
    )(a, w, b)


def _glu_kernel(a_ref, w1_ref, w2_ref, b1_ref, b2_ref, o_ref):
    a = a_ref[...]
    y1 = _dot(a, w1_ref[...]) + b1_ref[...]
    y2 = _dot(a, w2_ref[...]) + b2_ref[...]
    o_ref[...] = (y1 * jax.nn.sigmoid(y2)).astype(o_ref.dtype)


def _glu(a, w, b, col1, col2, ncols, tm, tn):
    m, k = a.shape
    c1, c2 = col1 // tn, col2 // tn
    return pl.pallas_call(
        _glu_kernel,
        grid=(ncols // tn, m // tm),
        in_specs=[
            pl.BlockSpec((tm, k), lambda j, i: (i, 0)),
            pl.BlockSpec((k, tn), lambda j, i: (0, j + c1)),
            pl.BlockSpec((k, tn), lambda j, i: (0, j + c2)),
            pl.BlockSpec((1, tn), lambda j, i: (0, j + c1)),
            pl.BlockSpec((1, tn), lambda j, i: (0, j + c2)),
        ],
        out_specs=pl.BlockSpec((tm, tn), lambda j, i: (i, j)),
        out_shape=jax.ShapeDtypeStruct((m, ncols), F32),
        compiler_params=_cparams("parallel", "parallel"),
        name="proj_glu",
    )(a, w, w, b, b)


def _rnn_kernel(xr_ref, cw_ref, cb_ref, wa_ref, wx_ref, ba_ref, bx_ref, lam_ref, gg_ref, o_ref,
                ext_ref, a_ref, u_ref, h_ref):
    ts, ct = xr_ref.shape
    kw = cw_ref.shape[0]
    s_idx = pl.program_id(2)

    @pl.when(s_idx == 0)
    def _():
        ext_ref[0:SUBLANES, :] = jnp.zeros((SUBLANES, ct), F32)
        h_ref[...] = jnp.zeros_like(h_ref)

    ext_ref[SUBLANES:SUBLANES + ts, :] = xr_ref[...]
    y = jnp.zeros((ts, ct), F32) + cb_ref[...]
    for k in range(kw):
        y = y + cw_ref[k:k + 1, :] * ext_ref[pl.ds(SUBLANES - (kw - 1) + k, ts), :]
    ext_ref[0:SUBLANES, :] = ext_ref[ts:ts + SUBLANES, :]

    nblk = ct // LANES
    r_parts, i_parts = [], []
    for blk in range(nblk):
        yb = y[:, blk * LANES:(blk + 1) * LANES].astype(BF16)
        r_parts.append(_dot(yb, wa_ref[blk]))
        i_parts.append(_dot(yb, wx_ref[blk]))
    r = jax.nn.sigmoid(jnp.concatenate(r_parts, axis=1) + ba_ref[...])
    ig = jax.nn.sigmoid(jnp.concatenate(i_parts, axis=1) + bx_ref[...])
    lam = lam_ref[...]
    sp = jnp.maximum(-lam, 0.0) + jnp.log1p(jnp.exp(-jnp.abs(lam)))
    log_a = (-LRU_C * r) * sp
    a = jnp.exp(log_a)
    u = jnp.sqrt(1.0 - jnp.exp(2.0 * log_a)) * (ig * y)
    a_ref[...] = a
    u_ref[...] = u

    row = lax.broadcasted_iota(jnp.int32, (SUBLANES, ct), 0)

    def body(g, h):
        r0 = pl.multiple_of(g * SUBLANES, SUBLANES)
        av = a_ref[pl.ds(r0, SUBLANES), :]
        uv = u_ref[pl.ds(r0, SUBLANES), :]
        for d in (1, 2, 4):
            a_sh = jnp.where(row >= d, pltpu.roll(av, d, 0), 1.0)
            u_sh = jnp.where(row >= d, pltpu.roll(uv, d, 0), 0.0)
            uv = av * u_sh + uv
            av = av * a_sh
        hh = uv + av * h
        u_ref[pl.ds(r0, SUBLANES), :] = hh
        return hh[SUBLANES - 1:SUBLANES, :]

    h_last = lax.fori_loop(0, ts // SUBLANES, body, h_ref[...])
    h_ref[...] = h_last
    o_ref[...] = (u_ref[...] * gg_ref[...].astype(F32)).astype(o_ref.dtype)


def _rnn_branch(xr, cw, cb, wa, wx, ba, bx, lam, gg, ts, ct):
    bsz, seq, c = xr.shape
    kw = cw.shape[0]
    nb = ct // LANES
    vec = pl.BlockSpec((1, ct), lambda b, j, s: (0, j))
    return pl.pallas_call(
        _rnn_kernel,
        grid=(bsz, c // ct, seq // ts),
        in_specs=[
            pl.BlockSpec((None, ts, ct), lambda b, j, s: (b, s, j)),
            pl.BlockSpec((kw, ct), lambda b, j, s: (0, j)),
            vec,
            pl.BlockSpec((nb, LANES, LANES), lambda b, j, s: (j, 0, 0)),
            pl.BlockSpec((nb, LANES, LANES), lambda b, j, s: (j, 0, 0)),
            vec, vec, vec,
            pl.BlockSpec((None, ts, ct), lambda b, j, s: (b, s, j)),
        ],
        out_specs=pl.BlockSpec((None, ts, ct), lambda b, j, s: (b, s, j)),
        out_shape=jax.ShapeDtypeStruct((bsz, seq, c), BF16),
        scratch_shapes=[
            pltpu.VMEM((ts + SUBLANES, ct), F32),
            pltpu.VMEM((ts, ct), F32),
            pltpu.VMEM((ts, ct), F32),
            pltpu.VMEM((1, ct), F32),
        ],
        compiler_params=_cparams("parallel", "parallel", "arbitrary"),
        name="rnn_branch",
    )(xr, cw, cb, wa, wx, ba, bx, lam, gg)


CONV_HALO = 32
CONV_LANE_CHUNK = 512


def _conf_kernel(c_ref, dw_ref, db_ref, g_ref, b_ref, o_ref, ext_ref, y_ref):
    ts, ch = c_ref.shape
    kw = dw_ref.shape[0]
    s_idx = pl.program_id(1)

    @pl.when(s_idx == 0)
    def _():
        ext_ref[0:CONV_HALO, :] = jnp.zeros((CONV_HALO, ch), F32)

    ext_ref[CONV_HALO:CONV_HALO + ts, :] = c_ref[...]
    off0 = CONV_HALO - (kw - 1)
    win_rows = CONV_HALO + SUBLANES
    lc = min(CONV_LANE_CHUNK, ch)
    n_lane = ch // lc

    def body(it, carry):
        rb = it // n_lane
        lb = it % n_lane
        r0 = pl.multiple_of(rb * SUBLANES, SUBLANES)
        l0 = pl.multiple_of(lb * lc, lc)
        win = ext_ref[pl.ds(r0, win_rows), pl.ds(l0, lc)]
        w = dw_ref[:, pl.ds(l0, lc)]
        acc = jnp.zeros((SUBLANES, lc), F32) + db_ref[:, pl.ds(l0, lc)]
        for k in range(kw):
            acc = acc + w[k:k + 1, :] * win[off0 + k:off0 + k + SUBLANES, :]
        y_ref[pl.ds(r0, SUBLANES), pl.ds(l0, lc)] = acc
        return carry

    lax.fori_loop(0, (ts // SUBLANES) * n_lane, body, 0)
    ext_ref[0:CONV_HALO, :] = ext_ref[ts:ts + CONV_HALO, :]
    yn = _layer_norm_rows(y_ref[...], g_ref[...], b_ref[...])
    o_ref[...] = (yn * jax.nn.sigmoid(yn)).astype(o_ref.dtype)


def _conf_branch(c, dw, db, g, b, ts):
    bsz, seq, ch = c.shape
    kw = dw.shape[0]
    assert kw - 1 <= CONV_HALO
    vec = pl.BlockSpec((1, ch), lambda bi, s: (0, 0))
    return pl.pallas_call(
        _conf_kernel,
        grid=(bsz, seq // ts),
        in_specs=[
            pl.BlockSpec((None, ts, ch), lambda bi, s: (bi, s, 0)),
            pl.BlockSpec((kw, ch), lambda bi, s: (0, 0)),
            vec, vec, vec,
        ],
        out_specs=pl.BlockSpec((None, ts, ch), lambda bi, s: (bi, s, 0)),
        out_shape=jax.ShapeDtypeStruct((bsz, seq, ch), BF16),
        scratch_shapes=[
            pltpu.VMEM((ts + CONV_HALO, ch), F32),
            pltpu.VMEM((ts, ch), F32),
        ],
        compiler_params=_cparams("parallel", "arbitrary"),
        name="conf_branch",
    )(c, dw, db, g, b)


def _merge_kernel(hr_ref, c_ref, wr_ref, wc_ref, bc_ref, gr_ref, gc_ref, o_ref):
    yr = _dot(hr_ref[...], wr_ref[...])
    yc = _dot(c_ref[...], wc_ref[...]) + bc_ref[...]
    o_ref[...] = (gr_ref[...].astype(F32) * yr + gc_ref[...].astype(F32) * yc).astype(o_ref.dtype)


def _merge(hr, cact, wr, wc, bc, gates, d_model, tm, tn):
    m, k = hr.shape
    goff = d_model // tn
    return pl.pallas_call(
        _merge_kernel,
        grid=(d_model // tn, m // tm),
        in_specs=[
            pl.BlockSpec((tm, k), lambda j, i: (i, 0)),
            pl.BlockSpec((tm, k), lambda j, i: (i, 0)),
            pl.BlockSpec((k, tn), lambda j, i: (0, j)),
            pl.BlockSpec((k, tn), lambda j, i: (0, j)),
            pl.BlockSpec((1, tn), lambda j, i: (0, j)),
            pl.BlockSpec((tm, tn), lambda j, i: (i, j)),
            pl.BlockSpec((tm, tn), lambda j, i: (i, j + goff)),
        ],
        out_specs=pl.BlockSpec((tm, tn), lambda j, i: (i, j)),
        out_shape=jax.ShapeDtypeStruct((m, d_model), BF16),
        compiler_params=_cparams("parallel", "parallel"),
        name="merge",
    )(hr, cact, wr, wc, bc, gates, gates)


def _mm_ln_kernel(a_ref, w_ref, res_ref, g_ref, b_ref, o_ref, obf_ref, *, alpha):
    y = _dot(a_ref[...], w_ref[...])
    z = _layer_norm_rows(alpha * res_ref[...] + y, g_ref[...], b_ref[...])
    o_ref[...] = z
    obf_ref[...] = z.astype(BF16)


def _mm_ln(a, w, res, g, b, alpha, tm):
    m, k = a.shape
    n = w.shape[1]
    vec = pl.BlockSpec((1, n), lambda i: (0, 0))
    return pl.pallas_call(
        functools.partial(_mm_ln_kernel, alpha=alpha),
        grid=(m // tm,),
        in_specs=[
            pl.BlockSpec((tm, k), lambda i: (i, 0)),
            pl.BlockSpec((k, n), lambda i: (0, 0)),
            pl.BlockSpec((tm, n), lambda i: (i, 0)),
            vec, vec,
        ],
        out_specs=[pl.BlockSpec((tm, n), lambda i: (i, 0)), pl.BlockSpec((tm, n), lambda i: (i, 0))],
        out_shape=[jax.ShapeDtypeStruct((m, n), F32), jax.ShapeDtypeStruct((m, n), BF16)],
        compiler_params=_cparams("parallel"),
        name="mm_ln",
    )(a, w, res, g, b)


def _add_ln_kernel(y_ref, res_ref, g_ref, b_ref, o_ref, *, alpha):
    o_ref[...] = _layer_norm_rows(alpha * res_ref[...] + y_ref[...], g_ref[...], b_ref[...])


def _add_ln(y, res, g, b, alpha, tm):
    m, n = y.shape
    vec = pl.BlockSpec((1, n), lambda i: (0, 0))
    return pl.pallas_call(
        functools.partial(_add_ln_kernel, alpha=alpha),
        grid=(m // tm,),
        in_specs=[pl.BlockSpec((tm, n), lambda i: (i, 0)), pl.BlockSpec((tm, n), lambda i: (i, 0)), vec, vec],
        out_specs=pl.BlockSpec((tm, n), lambda i: (i, 0)),
        out_shape=jax.ShapeDtypeStruct((m, n), F32),
        compiler_params=_cparams("parallel"),
        name="add_ln",
    )(y, res, g, b)


def _attn_kernel(q_ref, k_ref, v_ref, o_ref, *, heads):
    tq, d = q_ref.shape
    hd = d // heads
    scale = hd ** -0.5
    outs = []
    for h in range(heads):
        sl = slice(h * hd, (h + 1) * hd)
        s = lax.dot_general(q_ref[:, sl], k_ref[:, sl], (((1,), (1,)), ((), ())),
                            preferred_element_type=F32) * scale
        m = jnp.max(s, axis=-1, keepdims=True)
        p = jnp.exp(s - m)
        p = p / jnp.sum(p, axis=-1, keepdims=True)
        outs.append(_dot(p.astype(BF16), v_ref[:, sl]))
    o_ref[...] = jnp.concatenate(outs, axis=1).astype(o_ref.dtype)


def _attention(q, kv, bsz, seq, mem_len, d_model, tq):
    nq = seq // tq
    return pl.pallas_call(
        functools.partial(_attn_kernel, heads=XA_HEADS),
        grid=(bsz, nq),
        in_specs=[
            pl.BlockSpec((tq, d_model), lambda b, i: (b * nq + i, 0)),
            pl.BlockSpec((mem_len, d_model), lambda b, i: (b, 0)),
            pl.BlockSpec((mem_len, d_model), lambda b, i: (b, 1)),
        ],
        out_specs=pl.BlockSpec((tq, d_model), lambda b, i: (b * nq + i, 0)),
        out_shape=jax.ShapeDtypeStruct((bsz * seq, d_model), BF16),
        compiler_params=_cparams("parallel", "parallel"),
        name="xattn",
    )(q, kv, kv)


def _extract_topk(s, k):
    rows = s.shape[0]
    iota = lax.broadcasted_iota(jnp.int32, s.shape, 0)
    rank = jnp.full(s.shape, float(k), F32)
    vals = []
    for r in range(k):
        m = jnp.max(s, axis=0, keepdims=True)
        idx = jnp.min(jnp.where(s == m, iota, rows), axis=0, keepdims=True)
        hit = iota == idx
        rank = jnp.where(hit, float(r), rank)
        s = jnp.where(hit, -jnp.inf, s)
        vals.append(m)
    return vals, rank


def _peer_route_kernel(x_ref, wq_ref, keys_ref, c1_ref, e1_ref, r2_ref, e2_ref, *, heads, topk):
    n_keys, half = keys_ref.shape[1], keys_ref.shape[2]
    q = _dot(x_ref[...], wq_ref[...])
    nt = (((1,), (1,)), ((), ()))
    for h in range(heads):
        base = h * 2 * half
        s1 = lax.dot_general(keys_ref[0], q[:, base:base + half], nt, preferred_element_type=F32)
        s2 = lax.dot_general(keys_ref[1], q[:, base + half:base + 2 * half], nt, preferred_element_type=F32)
        v1, rank1 = _extract_topk(s1, topk)
        v2, rank2 = _extract_topk(s2, topk)
        v2_all = jnp.concatenate(v2, axis=0)
        cand = jnp.concatenate([v1[a] + v2_all for a in range(topk)], axis=0)
        _, rankc = _extract_topk(cand, topk)
        sel = rankc < float(topk)
        e1_top = [jnp.exp(v1[a] - v1[0]) for a in range(topk)]
        e2_top = jnp.exp(v2_all - v2[0])
        prod = jnp.concatenate([e1_top[a] * e2_top for a in range(topk)], axis=0)
        z = jnp.sum(jnp.where(sel, prod, 0.0), axis=0, keepdims=True)
        c1 = jnp.zeros_like(s1)
        for a in range(topk):
            cnt = jnp.sum(sel[a * topk:(a + 1) * topk].astype(F32), axis=0, keepdims=True)
            c1 = jnp.where(rank1 == float(a), cnt, c1)
        c1_ref[h] = _dup_bf16_words(c1)
        e1_ref[h] = _dup_bf16_words(jnp.exp(s1 - v1[0]))
        r2_ref[h] = rank2.astype(BF16)
        e2_ref[h] = (jnp.exp(s2 - v2[0]) / z).astype(BF16)


def _dup_bf16_words(v):
    b = lax.bitcast_convert_type(v.astype(BF16).astype(F32), jnp.uint32)
    return b | (b >> 16)


def _peer_route(x_bf, wq, keys, heads, tt):
    t, d = x_bf.shape
    n_keys = keys.shape[1]
    out_w = jax.ShapeDtypeStruct((heads, n_keys, t), jnp.uint32)
    out_b = jax.ShapeDtypeStruct((heads, n_keys, t), BF16)
    ospec = pl.BlockSpec((heads, n_keys, tt), lambda i: (0, 0, i))
    return pl.pallas_call(
        functools.partial(_peer_route_kernel, heads=heads, topk=PEER_TOPK),
        grid=(t // tt,),
        in_specs=[
            pl.BlockSpec((tt, d), lambda i: (i, 0)),
            pl.BlockSpec(wq.shape, lambda i: (0, 0)),
            pl.BlockSpec(keys.shape, lambda i: (0, 0, 0)),
        ],
        out_specs=[ospec, ospec, ospec, ospec],
        out_shape=[out_w, out_w, out_b, out_b],
        compiler_params=_cparams("parallel"),
        name="peer_route",
    )(x_bf, wq, keys)


def _peer_dense_kernel(xn_ref, un_ref, vt_ref, c1_ref, e1_ref, r2_ref, e2_ref, o_ref, acc_ref, ht_ref, act_ref, *,
                       heads, n_e):
    s_idx = pl.program_id(0)
    e_gate = jnp.maximum(s_idx - 1, 0) % n_e
    e_down = jnp.maximum(s_idx - 2, 0) % n_e
    ec = un_ref.shape[0]
    tt = xn_ref.shape[0]
    n_keys = r2_ref.shape[1]
    rows_per_step = ec // n_keys

    @pl.when(s_idx == 0)
    def _():
        ht_ref[...] = jnp.zeros_like(ht_ref)
        act_ref[...] = jnp.zeros_like(act_ref)

    @pl.when(e_down == 0)
    def _():
        acc_ref[...] = jnp.zeros_like(acc_ref)

    acc_ref[...] += _dot(vt_ref[...], act_ref[...])
    ht = ht_ref[...]
    ht_ref[...] = lax.dot_general(un_ref[...], xn_ref[...], (((1,), (1,)), ((), ())), preferred_element_type=F32)

    def row_bf16(ref, h, i):
        words = jnp.broadcast_to(ref[h, pl.ds(i, 1), :], (n_keys // 2, tt))
        return pltpu.bitcast(words, BF16)

    for ii in range(rows_per_step):
        i = e_gate * rows_per_step + ii
        g = None
        for h in range(heads):
            sel = jnp.where(r2_ref[h] < row_bf16(c1_ref, h, i), e2_ref[h], jnp.zeros((), BF16))
            term = sel * row_bf16(e1_ref, h, i)
            g = term if g is None else g + term
        rows = slice(ii * n_keys, (ii + 1) * n_keys)
        act_ref[rows, :] = jax.nn.gelu(ht[rows, :]).astype(BF16) * g

    @pl.when(jnp.logical_and(e_down == n_e - 1, s_idx > 1))
    def _():
        o_ref[...] = acc_ref[...].T


def _peer_dense(x_bf, u_bf, vt_bf, c1, e1, r2, e2, tt, ec):
    t, d = x_bf.shape
    n_exp = u_bf.shape[0]
    heads, n_keys, _ = c1.shape
    n_t, n_e = t // tt, n_exp // ec
    n_chunks = n_t * n_e
    up = lambda s: jnp.minimum(s, n_chunks - 1)
    gate = lambda s: jnp.clip(s - 1, 0, n_chunks - 1)
    down = lambda s: jnp.maximum(s - 2, 0)
    rspec = pl.BlockSpec((heads, n_keys, tt), lambda s: (0, 0, gate(s) // n_e))
    return pl.pallas_call(
        functools.partial(_peer_dense_kernel, heads=heads, n_e=n_e),
        grid=(n_chunks + 2,),
        in_specs=[
            pl.BlockSpec((tt, d), lambda s: (up(s) // n_e, 0)),
            pl.BlockSpec((ec, d), lambda s: (up(s) % n_e, 0)),
            pl.BlockSpec((d, ec), lambda s: (0, down(s) % n_e)),
            rspec, rspec, rspec, rspec,
        ],
        out_specs=pl.BlockSpec((tt, d), lambda s: (down(s) // n_e, 0)),
        out_shape=jax.ShapeDtypeStruct((t, d), F32),
        scratch_shapes=[pltpu.VMEM((d, tt), F32), pltpu.VMEM((ec, tt), F32), pltpu.VMEM((ec, tt), BF16)],
        compiler_params=_cparams("arbitrary"),
        name="peer_dense",
    )(x_bf, u_bf, vt_bf, c1, e1, r2, e2)


def _transpose_cast_kernel(v_ref, o_ref):
    o_ref[...] = v_ref[...].T.astype(o_ref.dtype)


def _transpose_cast(v, te):
    n_exp, d = v.shape
    return pl.pallas_call(
        _transpose_cast_kernel,
        grid=(n_exp // te,),
        in_specs=[pl.BlockSpec((te, d), lambda e: (e, 0))],
        out_specs=pl.BlockSpec((d, te), lambda e: (0, e)),
        out_shape=jax.ShapeDtypeStruct((d, n_exp), BF16),
        compiler_params=_cparams("parallel"),
        name="transpose_cast",
    )(v)


def _tile(n, pref):
    t = min(n, pref)
    assert n % t == 0
    return t


def _layer(x, mem, w_in, b_in, rnn_conv_w, rnn_conv_b, rnn_w_a, rnn_b_a, rnn_w_x, rnn_b_x, rnn_lambda,
           w_rnn_out, conf_dw_w, conf_dw_b, conf_ln_g, conf_ln_b, w_conf_out, b_conf_out, w_mix_out,
           ln1_g, ln1_b, xa_w_q, xa_w_k, xa_w_v, xa_w_o, ln2_g, ln2_b,
           peer_w_q, peer_sub_keys, peer_u, peer_v, ln3_g, ln3_b, alpha):
    bsz, seq, d = x.shape
    t = bsz * seq
    mem_len = mem.shape[1]
    d_rnn = rnn_conv_w.shape[1]
    d_conv = conf_dw_w.shape[1]
    row = lambda v: v.reshape(1, -1)

    tm = _tile(t, 1024)
    tn = _tile(d, 1024)
    x2d = x.reshape(t, d)
    x_bf = x2d.astype(BF16)
    w_in_bf = w_in.astype(BF16)
    b_in2 = row(b_in)

    xr = _proj(x_bf, w_in_bf, b_in2, 0, d_rnn, "none", F32, tm, tn)
    gg = _proj(x_bf, w_in_bf, b_in2, d_rnn, d_rnn, "gelu", BF16, tm, tn)
    cglu = _glu(x_bf, w_in_bf, b_in2, 2 * d_rnn, 2 * d_rnn + d_conv, d_conv, tm, tn)
    mgates = _proj(x_bf, w_in_bf, b_in2, 2 * d_rnn + 2 * d_conv, 2 * d, "sigmoid", BF16, tm, tn)

    hr = _rnn_branch(xr.reshape(bsz, seq, d_rnn), rnn_conv_w, row(rnn_conv_b),
                     rnn_w_a.astype(BF16), rnn_w_x.astype(BF16), row(rnn_b_a), row(rnn_b_x),
                     row(rnn_lambda), gg.reshape(bsz, seq, d_rnn),
                     _tile(seq, 512), _tile(d_rnn, 512))
    cact = _conf_branch(cglu.reshape(bsz, seq, d_conv), conf_dw_w, row(conf_dw_b),
                        row(conf_ln_g), row(conf_ln_b), _tile(seq, 256))

    merged = _merge(hr.reshape(t, d_rnn), cact.reshape(t, d_conv), w_rnn_out.astype(BF16),
                    w_conf_out.astype(BF16), row(b_conf_out), mgates, d, tm, tn)
    tl = _tile(t, 512)
    x1, x1_bf = _mm_ln(merged, w_mix_out.astype(BF16), x2d, row(ln1_g), row(ln1_b), alpha, tl)

    zero_d = jnp.zeros((1, d), F32)
    q = _proj(x1_bf, xa_w_q.astype(BF16), zero_d, 0, d, "none", BF16, tm, tn)
    w_kv = jnp.concatenate([xa_w_k, xa_w_v], axis=1).astype(BF16)
    mem_bf = mem.reshape(bsz * mem_len, d).astype(BF16)
    kv = _proj(mem_bf, w_kv, jnp.zeros((1, 2 * d), F32), 0, 2 * d, "none", BF16,
               _tile(bsz * mem_len, 1024), tn)
    o = _attention(q, kv, bsz, seq, mem_len, d, _tile(seq, 512))
    x2, x2_bf = _mm_ln(o, xa_w_o.astype(BF16), x1, row(ln2_g), row(ln2_b), alpha, tl)

    heads = peer_w_q.shape[1] // (2 * peer_sub_keys.shape[2])
    c1, e1, r2, e2 = _peer_route(x2_bf, peer_w_q.astype(BF16), peer_sub_keys, heads, _tile(t, 256))
    n_exp = peer_u.shape[0]
    ec = _tile(n_exp, 512)
    vt_bf = _transpose_cast(peer_v, ec)
    ff = _peer_dense(x2_bf, peer_u.astype(BF16), vt_bf, c1, e1, r2, e2, _tile(t, 512), ec)
    x3 = _add_ln(ff, x2, row(ln3_g), row(ln3_b), alpha, tl)
    return x3.reshape(bsz, seq, d)


def kernel(x, mem, w_in, b_in, rnn_conv_w, rnn_conv_b, rnn_w_a, rnn_b_a, rnn_w_x, rnn_b_x, rnn_lambda, w_rnn_out, conf_dw_w, conf_dw_b, conf_ln_g, conf_ln_b, w_conf_out, b_conf_out, w_mix_out, ln1_g, ln1_b, xa_w_q, xa_w_k, xa_w_v, xa_w_o, ln2_g, ln2_b, peer_w_q, peer_sub_keys, peer_u, peer_v, ln3_g, ln3_b):
    depth = w_in.shape[0]
    alpha = (2 * depth) ** 0.25
    params = (w_in, b_in, rnn_conv_w, rnn_conv_b, rnn_w_a, rnn_b_a, rnn_w_x, rnn_b_x, rnn_lambda, w_rnn_out,
              conf_dw_w, conf_dw_b, conf_ln_g, conf_ln_b, w_conf_out, b_conf_out, w_mix_out, ln1_g, ln1_b,
              xa_w_q, xa_w_k, xa_w_v, xa_w_o, ln2_g, ln2_b, peer_w_q, peer_sub_keys, peer_u, peer_v,
              ln3_g, ln3_b)
    for l in range(depth):
        x = _layer(x, mem, *[p[l] for p in params], alpha)
    return x
```

```python
import functools
import math

import jax
import jax.numpy as jnp
from jax import lax
from jax.experimental import pallas as pl
from jax.experimental.pallas import tpu as pltpu

F32 = jnp.float32
BF16 = jnp.bfloat16

LRU_C = 8.0
LN_EPS = 1e-5
XA_HEADS = 4
PEER_TOPK = 16
VMEM_LIMIT_BYTES = 56 * 1024 * 1024
LANES = 128
SUBLANES = 8


def _cparams(*sem):
    return pltpu.CompilerParams(dimension_semantics=sem, vmem_limit_bytes=VMEM_LIMIT_BYTES)


def _layer_norm_rows(z, g, b):
    mu = jnp.mean(z, axis=-1, keepdims=True)
    zc = z - mu
    var = jnp.mean(zc * zc, axis=-1, keepdims=True)
    return zc * lax.rsqrt(var + LN_EPS) * g + b


def _dot(a, b):
    return jnp.dot(a, b, preferred_element_type=F32)


def _proj_kernel(a_ref, w_ref, b_ref, o_ref, *, act):
    y = _dot(a_ref[...], w_ref[...]) + b_ref[...]
    if act == "gelu":
        y = jax.nn.gelu(y)
    elif act == "sigmoid":
        y = jax.nn.sigmoid(y)
    o_ref[...] = y.astype(o_ref.dtype)


def _proj(a, w, b, col0, ncols, act, out_dtype, tm, tn):
    m, k = a.shape
    c0 = col0 // tn
    return pl.pallas_call(
        functools.partial(_proj_kernel, act=act),
        grid=(ncols // tn, m // tm),
        in_specs=[
            pl.BlockSpec((tm, k), lambda j, i: (i, 0)),
            pl.BlockSpec((k, tn), lambda j, i: (0, j + c0)),
            pl.BlockSpec((1, tn), lambda j, i: (0, j + c0)),
        ],
        out_specs=pl.BlockSpec((tm, tn), lambda j, i: (i, j)),
        out_shape=jax.ShapeDtypeStruct((m, ncols), out_dtype),
        compiler_params=_cparams("parallel", "parallel"),
        name="proj_" + act,
    )(a, w, b)


def _glu_kernel(a_ref, w1_ref, w2_ref, b1_ref, b2_ref, o_ref):
    a = a_ref[...]
    y1 = _dot(a, w1_ref[...]) + b1_ref[...]
    y2 = _dot(a, w2_ref[...]) + b2_ref[...]
    o_ref[...] = (y1 * jax.nn.sigmoid(y2)).astype(o_ref.dtype)


def _glu(a, w, b, col1, col2, ncols, tm, tn):
    m, k = a.shape
    c1, c2 = col1 // tn, col2 // tn
    return pl.pallas_call(
        _glu_kernel,
        grid=(ncols // tn, m // tm),
        in_specs=[
            pl.BlockSpec((tm, k), lambda j, i: (i, 0)),
            pl.BlockSpec((k, tn), lambda j, i: (0, j + c1)),
            pl.BlockSpec((k, tn), lambda j, i: (0, j + c2)),
            pl.BlockSpec((1, tn), lambda j, i: (0, j + c1)),
            pl.BlockSpec((1, tn), lambda j, i: (0, j + c2)),
        ],
        out_specs=pl.BlockSpec((tm, tn), lambda j, i: (i, j)),
        out_shape=jax.ShapeDtypeStruct((m, ncols), F32),
        compiler_params=_cparams("parallel", "parallel"),
        name="proj_glu",
    )(a, w, w, b, b)


def _rnn_kernel(xr_ref, cw_ref, cb_ref, wa_ref, wx_ref, ba_ref, bx_ref, lam_ref, gg_ref, o_ref,
                ext_ref, a_ref, u_ref, h_ref):
    ts, ct = xr_ref.shape
    kw = cw_ref.shape[0]
    s_idx = pl.program_id(2)

    @pl.when(s_idx == 0)
    def _():
        ext_ref[0:SUBLANES, :] = jnp.zeros((SUBLANES, ct), F32)
        h_ref[...] = jnp.zeros_like(h_ref)

    ext_ref[SUBLANES:SUBLANES + ts, :] = xr_ref[...]
    y = jnp.zeros((ts, ct), F32) + cb_ref[...]
    ext = ext_ref[...]
    for k in range(kw):
        off = SUBLANES - (kw - 1) + k
        z = ext if off % SUBLANES == 0 else pltpu.roll(ext, ts + SUBLANES - off % SUBLANES, 0)
        base = off - off % SUBLANES
        y = y + cw_ref[k:k + 1, :] * z[base:base + ts, :]
    ext_ref[0:SUBLANES, :] = ext_ref[ts:ts + SUBLANES, :]

    nblk = ct // LANES
    r_parts, i_parts = [], []
    for blk in range(nblk):
        yb = y[:, blk * LANES:(blk + 1) * LANES].astype(BF16)
        r_parts.append(_dot(yb, wa_ref[blk]))
        i_parts.append(_dot(yb, wx_ref[blk]))
    r = jax.nn.sigmoid(jnp.concatenate(r_parts, axis=1) + ba_ref[...])
    ig = jax.nn.sigmoid(jnp.concatenate(i_parts, axis=1) + bx_ref[...])
    lam = lam_ref[...]
    sp = jnp.maximum(-lam, 0.0) + jnp.log1p(jnp.exp(-jnp.abs(lam)))
    log_a = (-LRU_C * r) * sp
    a = jnp.exp(log_a)
    u = jnp.sqrt(1.0 - jnp.exp(2.0 * log_a)) * (ig * y)
    a_ref[...] = a
    u_ref[...] = u

    row = lax.broadcasted_iota(jnp.int32, (SUBLANES, ct), 0)

    def body(g, h):
        r0 = pl.multiple_of(g * SUBLANES, SUBLANES)
        av = a_ref[pl.ds(r0, SUBLANES), :]
        uv = u_ref[pl.ds(r0, SUBLANES), :]
        for d in (1, 2, 4):
            a_sh = jnp.where(row >= d, pltpu.roll(av, d, 0), 1.0)
            u_sh = jnp.where(row >= d, pltpu.roll(uv, d, 0), 0.0)
            uv = av * u_sh + uv
            av = av * a_sh
        hh = uv + av * h
        u_ref[pl.ds(r0, SUBLANES), :] = hh
        return hh[SUBLANES - 1:SUBLANES, :]

    h_last = lax.fori_loop(0, ts // SUBLANES, body, h_ref[...], unroll=4)
    h_ref[...] = h_last
    o_ref[...] = (u_ref[...] * gg_ref[...].astype(F32)).astype(o_ref.dtype)


def _rnn_branch(xr, cw, cb, wa, wx, ba, bx, lam, gg, ts, ct):
    bsz, seq, c = xr.shape
    kw = cw.shape[0]
    assert kw - 1 <= SUBLANES
    nb = ct // LANES
    vec = pl.BlockSpec((1, ct), lambda b, j, s: (0, j))
    return pl.pallas_call(
        _rnn_kernel,
        grid=(bsz, c // ct, seq // ts),
        in_specs=[
            pl.BlockSpec((None, ts, ct), lambda b, j, s: (b, s, j)),
            pl.BlockSpec((kw, ct), lambda b, j, s: (0, j)),
            vec,
            pl.BlockSpec((nb, LANES, LANES), lambda b, j, s: (j, 0, 0)),
            pl.BlockSpec((nb, LANES, LANES), lambda b, j, s: (j, 0, 0)),
            vec, vec, vec,
            pl.BlockSpec((None, ts, ct), lambda b, j, s: (b, s, j)),
        ],
        out_specs=pl.BlockSpec((None, ts, ct), lambda b, j, s: (b, s, j)),
        out_shape=jax.ShapeDtypeStruct((bsz, seq, c), BF16),
        scratch_shapes=[
            pltpu.VMEM((ts + SUBLANES, ct), F32),
            pltpu.VMEM((ts, ct), F32),
            pltpu.VMEM((ts, ct), F32),
            pltpu.VMEM((1, ct), F32),
        ],
        compiler_params=_cparams("parallel", "parallel", "arbitrary"),
        name="rnn_branch",
    )(xr, cw, cb, wa, wx, ba, bx, lam, gg)


CONV_HALO = 32
CONV_LANE_CHUNK = 256
CONV_ROW_BLOCK = 32


def _conf_kernel(c_ref, dw_ref, db_ref, g_ref, b_ref, o_ref, ext_ref, y_ref):
    ts, ch = c_ref.shape
    kw = dw_ref.shape[0]
    s_idx = pl.program_id(1)

    @pl.when(s_idx == 0)
    def _():
        ext_ref[0:CONV_HALO, :] = jnp.zeros((CONV_HALO, ch), F32)

    ext_ref[CONV_HALO:CONV_HALO + ts, :] = c_ref[...]
    off0 = CONV_HALO - (kw - 1)
    rb_rows = min(CONV_ROW_BLOCK, ts)
    win_rows = CONV_HALO + rb_rows
    lc = min(CONV_LANE_CHUNK, ch)
    n_lane = ch // lc

    def body(it, carry):
        rb = it // n_lane
        lb = it % n_lane
        r0 = pl.multiple_of(rb * rb_rows, rb_rows)
        l0 = pl.multiple_of(lb * lc, lc)
        win = ext_ref[pl.ds(r0, win_rows), pl.ds(l0, lc)]
        acc = jnp.zeros((rb_rows, lc), F32) + db_ref[:, pl.ds(l0, lc)]
        for r in range(SUBLANES):
            z = win if r == 0 else pltpu.roll(win, win_rows - r, 0)
            for q in range(win_rows // SUBLANES):
                k = q * SUBLANES + r - off0
                if 0 <= k < kw:
                    acc = acc + dw_ref[k:k + 1, pl.ds(l0, lc)] * z[q * SUBLANES:q * SUBLANES + rb_rows, :]
        y_ref[pl.ds(r0, rb_rows), pl.ds(l0, lc)] = acc
        return carry

    lax.fori_loop(0, (ts // rb_rows) * n_lane, body, 0)
    ext_ref[0:CONV_HALO, :] = ext_ref[ts:ts + CONV_HALO, :]
    yn = _layer_norm_rows(y_ref[...], g_ref[...], b_ref[...])
    o_ref[...] = (yn * jax.nn.sigmoid(yn)).astype(o_ref.dtype)


def _conf_branch(c, dw, db, g, b, ts):
    bsz, seq, ch = c.shape
    kw = dw.shape[0]
    assert kw - 1 <= CONV_HALO
    vec = pl.BlockSpec((1, ch), lambda bi, s: (0, 0))
    return pl.pallas_call(
        _conf_kernel,
        grid=(bsz, seq // ts),
        in_specs=[
            pl.BlockSpec((None, ts, ch), lambda bi, s: (bi, s, 0)),
            pl.BlockSpec((kw, ch), lambda bi, s: (0, 0)),
            vec, vec, vec,
        ],
        out_specs=pl.BlockSpec((None, ts, ch), lambda bi, s: (bi, s, 0)),
        out_shape=jax.ShapeDtypeStruct((bsz, seq, ch), BF16),
        scratch_shapes=[
            pltpu.VMEM((ts + CONV_HALO, ch), F32),
            pltpu.VMEM((ts, ch), F32),
        ],
        compiler_params=_cparams("parallel", "arbitrary"),
        name="conf_branch",
    )(c, dw, db, g, b)


def _merge_kernel(hr_ref, c_ref, wr_ref, wc_ref, bc_ref, gr_ref, gc_ref, o_ref):
    yr = _dot(hr_ref[...], wr_ref[...])
    yc = _dot(c_ref[...], wc_ref[...]) + bc_ref[...]
    o_ref[...] = (gr_ref[...].astype(F32) * yr + gc_ref[...].astype(F32) * yc).astype(o_ref.dtype)


def _merge(hr, cact, wr, wc, bc, gates, d_model, tm, tn):
    m, k = hr.shape
    goff = d_model // tn
    return pl.pallas_call(
        _merge_kernel,
        grid=(d_model // tn, m // tm),
        in_specs=[
            pl.BlockSpec((tm, k), lambda j, i: (i, 0)),
            pl.BlockSpec((tm, k), lambda j, i: (i, 0)),
            pl.BlockSpec((k, tn), lambda j, i: (0, j)),
            pl.BlockSpec((k, tn), lambda j, i: (0, j)),
            pl.BlockSpec((1, tn), lambda j, i: (0, j)),
            pl.BlockSpec((tm, tn), lambda j, i: (i, j)),
            pl.BlockSpec((tm, tn), lambda j, i: (i, j + goff)),
        ],
        out_specs=pl.BlockSpec((tm, tn), lambda j, i: (i, j)),
        out_shape=jax.ShapeDtypeStruct((m, d_model), BF16),
        compiler_params=_cparams("parallel", "parallel"),
        name="merge",
    )(hr, cact, wr, wc, bc, gates, gates)


def _mm_ln_kernel(a_ref, w_ref, res_ref, g_ref, b_ref, o_ref, obf_ref, *, alpha):
    y = _dot(a_ref[...], w_ref[...])
    z = _layer_norm_rows(alpha * res_ref[...] + y, g_ref[...], b_ref[...])
    o_ref[...] = z
    obf_ref[...] = z.astype(BF16)


def _mm_ln(a, w, res, g, b, alpha, tm):
    m, k = a.shape
    n = w.shape[1]
    vec = pl.BlockSpec((1, n), lambda i: (0, 0))
    return pl.pallas_call(
        functools.partial(_mm_ln_kernel, alpha=alpha),
        grid=(m // tm,),
        in_specs=[
            pl.BlockSpec((tm, k), lambda i: (i, 0)),
            pl.BlockSpec((k, n), lambda i: (0, 0)),
            pl.BlockSpec((tm, n), lambda i: (i, 0)),
            vec, vec,
        ],
        out_specs=[pl.BlockSpec((tm, n), lambda i: (i, 0)), pl.BlockSpec((tm, n), lambda i: (i, 0))],
        out_shape=[jax.ShapeDtypeStruct((m, n), F32), jax.ShapeDtypeStruct((m, n), BF16)],
        compiler_params=_cparams("parallel"),
        name="mm_ln",
    )(a, w, res, g, b)


def _add_ln_kernel(y_ref, res_ref, g_ref, b_ref, o_ref, *, alpha):
    o_ref[...] = _layer_norm_rows(alpha * res_ref[...] + y_ref[...], g_ref[...], b_ref[...])


def _add_ln(y, res, g, b, alpha, tm):
    m, n = y.shape
    vec = pl.BlockSpec((1, n), lambda i: (0, 0))
    return pl.pallas_call(
        functools.partial(_add_ln_kernel, alpha=alpha),
        grid=(m // tm,),
        in_specs=[pl.BlockSpec((tm, n), lambda i: (i, 0)), pl.BlockSpec((tm, n), lambda i: (i, 0)), vec, vec],
        out_specs=pl.BlockSpec((tm, n), lambda i: (i, 0)),
        out_shape=jax.ShapeDtypeStruct((m, n), F32),
        compiler_params=_cparams("parallel"),
        name="add_ln",
    )(y, res, g, b)


def _attn_kernel(q_ref, k_ref, v_ref, o_ref, *, heads):
    tq, d = q_ref.shape
    hd = d // heads
    scale = hd ** -0.5
    outs = []
    for h in range(heads):
        sl = slice(h * hd, (h + 1) * hd)
        s = lax.dot_general(q_ref[:, sl], k_ref[:, sl], (((1,), (1,)), ((), ())),
                            preferred_element_type=F32) * scale
        m = jnp.max(s, axis=-1, keepdims=True)
        p = jnp.exp(s - m)
        p = p / jnp.sum(p, axis=-1, keepdims=True)
        outs.append(_dot(p.astype(BF16), v_ref[:, sl]))
    o_ref[...] = jnp.concatenate(outs, axis=1).astype(o_ref.dtype)


def _attention(q, kv, bsz, seq, mem_len, d_model, tq):
    nq = seq // tq
    return pl.pallas_call(
        functools.partial(_attn_kernel, heads=XA_HEADS),
        grid=(bsz, nq),
        in_specs=[
            pl.BlockSpec((tq, d_model), lambda b, i: (b * nq + i, 0)),
            pl.BlockSpec((mem_len, d_model), lambda b, i: (b, 0)),
            pl.BlockSpec((mem_len, d_model), lambda b, i: (b, 1)),
        ],
        out_specs=pl.BlockSpec((tq, d_model), lambda b, i: (b * nq + i, 0)),
        out_shape=jax.ShapeDtypeStruct((bsz * seq, d_model), BF16),
        compiler_params=_cparams("parallel", "parallel"),
        name="xattn",
    )(q, kv, kv)


def _extract_topk(s, k, pos, exact):
    rank = jnp.full(s.shape, float(k), F32)
    vals = []
    for r in range(k):
        m = jnp.max(s, axis=0, keepdims=True)
        hit = s == m
        if exact:
            first = jnp.min(jnp.where(hit, pos, jnp.iinfo(jnp.int32).max), axis=0, keepdims=True)
            hit = pos == first
        rank = jnp.where(hit, float(r), rank)
        s = jnp.where(hit, -jnp.inf, s)
        vals.append(m)
    n_ranked = jnp.sum(jnp.where(rank < float(k), 1.0, 0.0), axis=0, keepdims=True)
    return vals, rank, n_ranked == float(k)


def _candidate_blocks(k):
    blocks = [("row", 0, k), ("row", 1, k // 2)]
    b = 0
    while 3 * (b + 1) <= k:
        hi = k // (b + 1)
        blocks.append(("col", b, -(-hi // SUBLANES) * SUBLANES, 2, hi))
        b += 1
    return blocks


def _route_head(s1, s2, k, exact):
    tt = s1.shape[1]
    key_pos = lax.broadcasted_iota(jnp.int32, s1.shape, 0)
    v1, rank1, ok1 = _extract_topk(s1, k, key_pos, exact)
    v2, rank2, ok2 = _extract_topk(s2, k, key_pos, exact)
    v1_all = jnp.concatenate(v1, axis=0)
    v2_all = jnp.concatenate(v2, axis=0)
    e1_all = jnp.exp(v1_all - v1[0])
    e2_all = jnp.exp(v2_all - v2[0])
    cands, prods, poss = [], [], []
    for blk in _candidate_blocks(k):
        if blk[0] == "row":
            _, a, n = blk
            cands.append(v1[a] + v2_all[0:n])
            prods.append(e1_all[a:a + 1] * e2_all[0:n])
            poss.append(a * k + lax.broadcasted_iota(jnp.int32, (n, tt), 0))
        else:
            _, b, n, lo, hi = blk
            a_col = lax.broadcasted_iota(jnp.int32, (n, tt), 0)
            valid = jnp.logical_and(a_col >= lo, a_col < hi)
            cands.append(jnp.where(valid, v1_all[0:n] + v2[b], -jnp.inf))
            prods.append(e1_all[0:n] * e2_all[b:b + 1])
            poss.append(jnp.where(valid, a_col * k + b, jnp.iinfo(jnp.int32).max))
    _, rankc, okc = _extract_topk(jnp.concatenate(cands, axis=0), k, jnp.concatenate(poss, axis=0), exact)
    sel = jnp.where(rankc < float(k), 1.0, 0.0)
    z = jnp.sum(sel * jnp.concatenate(prods, axis=0), axis=0, keepdims=True)
    cnt = jnp.zeros((k, s1.shape[1]), F32)
    a_col = lax.broadcasted_iota(jnp.int32, (k, tt), 0)
    row0 = 0
    for blk in _candidate_blocks(k):
        n = blk[2]
        part = sel[row0:row0 + n]
        row0 += n
        if blk[0] == "row":
            cnt = cnt + jnp.where(a_col == blk[1], jnp.sum(part, axis=0, keepdims=True), 0.0)
        elif n == k:
            cnt = cnt + part
        else:
            cnt = cnt + jnp.concatenate([part, jnp.zeros((k - n, part.shape[1]), F32)], axis=0)
    c1 = jnp.zeros_like(s1)
    for a in range(k):
        c1 = jnp.where(rank1 == float(a), cnt[a:a + 1], c1)
    ok = jnp.logical_and(jnp.logical_and(ok1, ok2), okc)
    return c1, jnp.exp(s1 - v1[0]), rank2, jnp.exp(s2 - v2[0]) / z, ok


def _peer_route_kernel(x_ref, wq_ref, keys_ref, c1_ref, e1_ref, r2_ref, e2_ref, *, heads, topk):
    n_keys, half = keys_ref.shape[1], keys_ref.shape[2]
    q = _dot(x_ref[...], wq_ref[...])
    nt = (((1,), (1,)), ((), ()))

    def route_all(exact):
        ok = None
        for h in range(heads):
            base = h * 2 * half
            s1 = lax.dot_general(keys_ref[0], q[:, base:base + half], nt, preferred_element_type=F32)
            s2 = lax.dot_general(keys_ref[1], q[:, base + half:base + 2 * half], nt, preferred_element_type=F32)
            c1, e1, rank2, e2, ok_h = _route_head(s1, s2, topk, exact)
            c1_ref[h] = _dup_bf16_words(c1)
            e1_ref[h] = _dup_bf16_words(e1)
            r2_ref[h] = rank2.astype(BF16)
            e2_ref[h] = e2.astype(BF16)
            ok = ok_h if ok is None else jnp.logical_and(ok, ok_h)
        return ok

    ok = route_all(exact=False)
    n_bad = jnp.sum(jnp.where(ok, 0.0, 1.0))

    @pl.when(n_bad > 0.0)
    def _():
        route_all(exact=True)


def _dup_bf16_words(v):
    b = lax.bitcast_convert_type(v.astype(BF16).astype(F32), jnp.uint32)
    return b | (b >> 16)


def _peer_route(x_bf, wq, keys, heads, tt):
    t, d = x_bf.shape
    n_keys = keys.shape[1]
    out_w = jax.ShapeDtypeStruct((heads, n_keys, t), jnp.uint32)
    out_b = jax.ShapeDtypeStruct((heads, n_keys, t), BF16)
    ospec = pl.BlockSpec((heads, n_keys, tt), lambda i: (0, 0, i))
    return pl.pallas_call(
        functools.partial(_peer_route_kernel, heads=heads, topk=PEER_TOPK),
        grid=(t // tt,),
        in_specs=[
            pl.BlockSpec((tt, d), lambda i: (i, 0)),
            pl.BlockSpec(wq.shape, lambda i: (0, 0)),
            pl.BlockSpec(keys.shape, lambda i: (0, 0, 0)),
        ],
        out_specs=[ospec, ospec, ospec, ospec],
        out_shape=[out_w, out_w, out_b, out_b],
        compiler_params=_cparams("parallel"),
        name="peer_route",
    )(x_bf, wq, keys)


def _peer_dense_kernel(xn_ref, un_ref, vt_ref, c1_ref, e1_ref, r2_ref, e2_ref, o_ref, acc_ref, ht_ref, act_ref, *,
                       heads, n_e):
    s_idx = pl.program_id(0)
    e_gate = jnp.maximum(s_idx - 1, 0) % n_e
    e_down = jnp.maximum(s_idx - 2, 0) % n_e
    ec = un_ref.shape[0]
    tt = xn_ref.shape[0]
    n_keys = r2_ref.shape[1]
    rows_per_step = ec // n_keys

    @pl.when(s_idx == 0)
    def _():
        ht_ref[...] = jnp.zeros_like(ht_ref)
        act_ref[...] = jnp.zeros_like(act_ref)

    @pl.when(e_down == 0)
    def _():
        acc_ref[...] = jnp.zeros_like(acc_ref)

    acc_ref[...] += _dot(vt_ref[...], act_ref[...])
    ht = ht_ref[...]
    ht_ref[...] = lax.dot_general(un_ref[...], xn_ref[...], (((1,), (1,)), ((), ())), preferred_element_type=F32)

    def row_bf16(ref, h, i):
        words = jnp.broadcast_to(ref[h, pl.ds(i, 1), :], (n_keys // 2, tt))
        return pltpu.bitcast(words, BF16)

    for ii in range(rows_per_step):
        i = e_gate * rows_per_step + ii
        g = None
        for h in range(heads):
            sel = jnp.where(r2_ref[h] < row_bf16(c1_ref, h, i), e2_ref[h], jnp.zeros((), BF16))
            term = sel * row_bf16(e1_ref, h, i)
            g = term if g is None else g + term
        rows = slice(ii * n_keys, (ii + 1) * n_keys)
        act_ref[rows, :] = jax.nn.gelu(ht[rows, :]).astype(BF16) * g

    @pl.when(jnp.logical_and(e_down == n_e - 1, s_idx > 1))
    def _():
        o_ref[...] = acc_ref[...].T


def _peer_dense(x_bf, u_bf, vt_bf, c1, e1, r2, e2, tt, ec):
    t, d = x_bf.shape
    n_exp = u_bf.shape[0]
    heads, n_keys, _ = c1.shape
    n_t, n_e = t // tt, n_exp // ec
    n_chunks = n_t * n_e
    up = lambda s: jnp.minimum(s, n_chunks - 1)
    gate = lambda s: jnp.clip(s - 1, 0, n_chunks - 1)
    down = lambda s: jnp.maximum(s - 2, 0)
    rspec = pl.BlockSpec((heads, n_keys, tt), lambda s: (0, 0, gate(s) // n_e))
    return pl.pallas_call(
        functools.partial(_peer_dense_kernel, heads=heads, n_e=n_e),
        grid=(n_chunks + 2,),
        in_specs=[
            pl.BlockSpec((tt, d), lambda s: (up(s) // n_e, 0)),
            pl.BlockSpec((ec, d), lambda s: (up(s) % n_e, 0)),
            pl.BlockSpec((None, d, ec), lambda s: (down(s) % n_e, 0, 0)),
            rspec, rspec, rspec, rspec,
        ],
        out_specs=pl.BlockSpec((tt, d), lambda s: (down(s) // n_e, 0)),
        out_shape=jax.ShapeDtypeStruct((t, d), F32),
        scratch_shapes=[pltpu.VMEM((d, tt), F32), pltpu.VMEM((ec, tt), F32), pltpu.VMEM((ec, tt), BF16)],
        compiler_params=_cparams("arbitrary"),
        name="peer_dense",
    )(x_bf, u_bf, vt_bf, c1, e1, r2, e2)


def _transpose_cast_kernel(v_ref, o_ref):
    o_ref[...] = v_ref[...].T.astype(o_ref.dtype)


def _transpose_cast(v, te):
    n_exp, d = v.shape
    return pl.pallas_call(
        _transpose_cast_kernel,
        grid=(n_exp // te,),
        in_specs=[pl.BlockSpec((te, d), lambda e: (e, 0))],
        out_specs=pl.BlockSpec((None, d, te), lambda e: (e, 0, 0)),
        out_shape=jax.ShapeDtypeStruct((n_exp // te, d, te), BF16),
        compiler_params=_cparams("parallel"),
        name="transpose_cast",
    )(v)


def _tile(n, pref):
    t = min(n, pref)
    assert n % t == 0
    return t


def _layer(x, mem, w_in, b_in, rnn_conv_w, rnn_conv_b, rnn_w_a, rnn_b_a, rnn_w_x, rnn_b_x, rnn_lambda,
           w_rnn_out, conf_dw_w, conf_dw_b, conf_ln_g, conf_ln_b, w_conf_out, b_conf_out, w_mix_out,
           ln1_g, ln1_b, xa_w_q, xa_w_k, xa_w_v, xa_w_o, ln2_g, ln2_b,
           peer_w_q, peer_sub_keys, peer_u, peer_v, ln3_g, ln3_b, alpha):
    bsz, seq, d = x.shape
    t = bsz * seq
    mem_len = mem.shape[1]
    d_rnn = rnn_conv_w.shape[1]
    d_conv = conf_dw_w.shape[1]
    row = lambda v: v.reshape(1, -1)

    tm = _tile(t, 1024)
    tn = _tile(d, 1024)
    x2d = x.reshape(t, d)
    x_bf = x2d.astype(BF16)
    w_in_bf = w_in.astype(BF16)
    b_in2 = row(b_in)

    xr = _proj(x_bf, w_in_bf, b_in2, 0, d_rnn, "none", F32, tm, tn)
    gg = _proj(x_bf, w_in_bf, b_in2, d_rnn, d_rnn, "gelu", BF16, tm, tn)
    cglu = _glu(x_bf, w_in_bf, b_in2, 2 * d_rnn, 2 * d_rnn + d_conv, d_conv, tm, tn)
    mgates = _proj(x_bf, w_in_bf, b_in2, 2 * d_rnn + 2 * d_conv, 2 * d, "sigmoid", BF16, tm, tn)

    hr = _rnn_branch(xr.reshape(bsz, seq, d_rnn), rnn_conv_w, row(rnn_conv_b),
                     rnn_w_a.astype(BF16), rnn_w_x.astype(BF16), row(rnn_b_a), row(rnn_b_x),
                     row(rnn_lambda), gg.reshape(bsz, seq, d_rnn),
                     _tile(seq, 512), _tile(d_rnn, 512))
    cact = _conf_branch(cglu.reshape(bsz, seq, d_conv), conf_dw_w, row(conf_dw_b),
                        row(conf_ln_g), row(conf_ln_b), _tile(seq, 256))

    merged = _merge(hr.reshape(t, d_rnn), cact.reshape(t, d_conv), w_rnn_out.astype(BF16),
                    w_conf_out.astype(BF16), row(b_conf_out), mgates, d, tm, tn)
    tl = _tile(t, 512)
    x1, x1_bf = _mm_ln(merged, w_mix_out.astype(BF16), x2d, row(ln1_g), row(ln1_b), alpha, tl)

    zero_d = jnp.zeros((1, d), F32)
    q = _proj(x1_bf, xa_w_q.astype(BF16), zero_d, 0, d, "none", BF16, tm, tn)
    w_kv = jnp.concatenate([xa_w_k, xa_w_v], axis=1).astype(BF16)
    mem_bf = mem.reshape(bsz * mem_len, d).astype(BF16)
    kv = _proj(mem_bf, w_kv, jnp.zeros((1, 2 * d), F32), 0, 2 * d, "none", BF16,
               _tile(bsz * mem_len, 1024), tn)
    o = _attention(q, kv, bsz, seq, mem_len, d, _tile(seq, 512))
    x2, x2_bf = _mm_ln(o, xa_w_o.astype(BF16), x1, row(ln2_g), row(ln2_b), alpha, tl)

    heads = peer_w_q.shape[1] // (2 * peer_sub_keys.shape[2])
    c1, e1, r2, e2 = _peer_route(x2_bf, peer_w_q.astype(BF16), peer_sub_keys, heads, _tile(t, 256))
    n_exp = peer_u.shape[0]
    ec = _tile(n_exp, 512)
    vt_bf = _transpose_cast(peer_v, ec)
    ff = _peer_dense(x2_bf, peer_u.astype(BF16), vt_bf, c1, e1, r2, e2, _tile(t, 512), ec)
    x3 = _add_ln(ff, x2, row(ln3_g), row(ln3_b), alpha, tl)
    return x3.reshape(bsz, seq, d)


def kernel(x, mem, w_in, b_in, rnn_conv_w, rnn_conv_b, rnn_w_a, rnn_b_a, rnn_w_x, rnn_b_x, rnn_lambda, w_rnn_out, conf_dw_w, conf_dw_b, conf_ln_g, conf_ln_b, w_conf_out, b_conf_out, w_mix_out, ln1_g, ln1_b, xa_w_q, xa_w_k, xa_w_v, xa_w_o, ln2_g, ln2_b, peer_w_q, peer_sub_keys, peer_u, peer_v, ln3_g, ln3_b):
    depth = w_in.shape[0]
    alpha = (2 * depth) ** 0.25
    params = (w_in, b_in, rnn_conv_w, rnn_conv_b, rnn_w_a, rnn_b_a, rnn_w_x, rnn_b_x, rnn_lambda, w_rnn_out,
              conf_dw_w, conf_dw_b, conf_ln_g, conf_ln_b, w_conf_out, b_conf_out, w_mix_out, ln1_g, ln1_b,
              xa_w_q, xa_w_k, xa_w_v, xa_w_o, ln2_g, ln2_b, peer_w_q, peer_sub_keys, peer_u, peer_v,
              ln3_g, ln3_b)
    for l in range(depth):
        x = _layer(x, mem, *[p[l] for p in params], alpha)
    return x
```

```python
import functools
import math

import jax
import jax.numpy as jnp
from jax import lax
from jax.experimental import pallas as pl
from jax.experimental.pallas import tpu as pltpu

F32 = jnp.float32
BF16 = jnp.bfloat16

LRU_C = 8.0
LN_EPS = 1e-5
XA_HEADS = 4
PEER_TOPK = 16
VMEM_LIMIT_BYTES = 56 * 1024 * 1024
LANES = 128
SUBLANES = 8


def _cparams(*sem):
    return pltpu.CompilerParams(dimension_semantics=sem, vmem_limit_bytes=VMEM_LIMIT_BYTES)


def _layer_norm_rows(z, g, b):
    mu = jnp.mean(z, axis=-1, keepdims=True)
    zc = z - mu
    var = jnp.mean(zc * zc, axis=-1, keepdims=True)
    return zc * lax.rsqrt(var + LN_EPS) * g + b


def _dot(a, b):
    return jnp.dot(a, b, preferred_element_type=F32)


def _proj_kernel(a_ref, w_ref, b_ref, o_ref, *, act):
    y = _dot(a_ref[...], w_ref[...]) + b_ref[...]
    if act == "gelu":
        y = jax.nn.gelu(y)
    elif act == "sigmoid":
        y = jax.nn.sigmoid(y)
    o_ref[...] = y.astype(o_ref.dtype)


def _proj(a, w, b, col0, ncols, act, out_dtype, tm, tn):
    m, k = a.shape
    c0 = col0 // tn
    return pl.pallas_call(
        functools.partial(_proj_kernel, act=act),
        grid=(ncols // tn, m // tm),
        in_specs=[
            pl.BlockSpec((tm, k), lambda j, i: (i, 0)),
            pl.BlockSpec((k, tn), lambda j, i: (0, j + c0)),
            pl.BlockSpec((1, tn), lambda j, i: (0, j + c0)),
        ],
        out_specs=pl.BlockSpec((tm, tn), lambda j, i: (i, j)),
        out_shape=jax.ShapeDtypeStruct((m, ncols), out_dtype),
        compiler_params=_cparams("parallel", "parallel"),
        name="proj_" + act,
    )(a, w, b)


def _glu_kernel(a_ref, w1_ref, w2_ref, b1_ref, b2_ref, o_ref):
    a = a_ref[...]
    y1 = _dot(a, w1_ref[...]) + b1_ref[...]
    y2 = _dot(a, w2_ref[...]) + b2_ref[...]
    o_ref[...] = (y1 * jax.nn.sigmoid(y2)).astype(o_ref.dtype)


def _glu(a, w, b, col1, col2, ncols, tm, tn):
    m, k = a.shape
    c1, c2 = col1 // tn, col2 // tn
    return pl.pallas_call(
        _glu_kernel,
        grid=(ncols // tn, m // tm),
        in_specs=[
            pl.BlockSpec((tm, k), lambda j, i: (i, 0)),
            pl.BlockSpec((k, tn), lambda j, i: (0, j + c1)),
            pl.BlockSpec((k, tn), lambda j, i: (0, j + c2)),
            pl.BlockSpec((1, tn), lambda j, i: (0, j + c1)),
            pl.BlockSpec((1, tn), lambda j, i: (0, j + c2)),
        ],
        out_specs=pl.BlockSpec((tm, tn), lambda j, i: (i, j)),
        out_shape=jax.ShapeDtypeStruct((m, ncols), F32),
        compiler_params=_cparams("parallel", "parallel"),
        name="proj_glu",
    )(a, w, w, b, b)


def _rnn_kernel(xr_ref, cw_ref, cb_ref, wa_ref, wx_ref, ba_ref, bx_ref, lam_ref, gg_ref, o_ref,
                ext_ref, a_ref, u_ref, h_ref):
    ts, ct = xr_ref.shape
    kw = cw_ref.shape[0]
    s_idx = pl.program_id(2)

    @pl.when(s_idx == 0)
    def _():
        ext_ref[0:SUBLANES, :] = jnp.zeros((SUBLANES, ct), F32)
        h_ref[...] = jnp.zeros_like(h_ref)

    ext_ref[SUBLANES:SUBLANES + ts, :] = xr_ref[...]
    y = jnp.zeros((ts, ct), F32) + cb_ref[...]
    ext = ext_ref[...]
    for k in range(kw):
        off = SUBLANES - (kw - 1) + k
        z = ext if off % SUBLANES == 0 else pltpu.roll(ext, ts + SUBLANES - off % SUBLANES, 0)
        base = off - off % SUBLANES
        y = y + cw_ref[k:k + 1, :] * z[base:base + ts, :]
    ext_ref[0:SUBLANES, :] = ext_ref[ts:ts + SUBLANES, :]

    nblk = ct // LANES
    r_parts, i_parts = [], []
    for blk in range(nblk):
        yb = y[:, blk * LANES:(blk + 1) * LANES].astype(BF16)
        r_parts.append(_dot(yb, wa_ref[blk]))
        i_parts.append(_dot(yb, wx_ref[blk]))
    r = jax.nn.sigmoid(jnp.concatenate(r_parts, axis=1) + ba_ref[...])
    ig = jax.nn.sigmoid(jnp.concatenate(i_parts, axis=1) + bx_ref[...])
    lam = lam_ref[...]
    sp = jnp.maximum(-lam, 0.0) + jnp.log1p(jnp.exp(-jnp.abs(lam)))
    log_a = (-LRU_C * r) * sp
    a = jnp.exp(log_a)
    u = jnp.sqrt(1.0 - jnp.exp(2.0 * log_a)) * (ig * y)
    a_ref[...] = a
    u_ref[...] = u

    row = lax.broadcasted_iota(jnp.int32, (SUBLANES, ct), 0)

    def body(g, h):
        r0 = pl.multiple_of(g * SUBLANES, SUBLANES)
        av = a_ref[pl.ds(r0, SUBLANES), :]
        uv = u_ref[pl.ds(r0, SUBLANES), :]
        for d in (1, 2, 4):
            a_sh = jnp.where(row >= d, pltpu.roll(av, d, 0), 1.0)
            u_sh = jnp.where(row >= d, pltpu.roll(uv, d, 0), 0.0)
            uv = av * u_sh + uv
            av = av * a_sh
        hh = uv + av * h
        u_ref[pl.ds(r0, SUBLANES), :] = hh
        return hh[SUBLANES - 1:SUBLANES, :]

    h_last = lax.fori_loop(0, ts // SUBLANES, body, h_ref[...], unroll=4)
    h_ref[...] = h_last
    o_ref[...] = (u_ref[...] * gg_ref[...].astype(F32)).astype(o_ref.dtype)


def _rnn_branch(xr, cw, cb, wa, wx, ba, bx, lam, gg, ts, ct):
    bsz, seq, c = xr.shape
    kw = cw.shape[0]
    assert kw - 1 <= SUBLANES
    nb = ct // LANES
    vec = pl.BlockSpec((1, ct), lambda b, j, s: (0, j))
    return pl.pallas_call(
        _rnn_kernel,
        grid=(bsz, c // ct, seq // ts),
        in_specs=[
            pl.BlockSpec((None, ts, ct), lambda b, j, s: (b, s, j)),
            pl.BlockSpec((kw, ct), lambda b, j, s: (0, j)),
            vec,
            pl.BlockSpec((nb, LANES, LANES), lambda b, j, s: (j, 0, 0)),
            pl.BlockSpec((nb, LANES, LANES), lambda b, j, s: (j, 0, 0)),
            vec, vec, vec,
            pl.BlockSpec((None, ts, ct), lambda b, j, s: (b, s, j)),
        ],
        out_specs=pl.BlockSpec((None, ts, ct), lambda b, j, s: (b, s, j)),
        out_shape=jax.ShapeDtypeStruct((bsz, seq, c), BF16),
        scratch_shapes=[
            pltpu.VMEM((ts + SUBLANES, ct), F32),
            pltpu.VMEM((ts, ct), F32),
            pltpu.VMEM((ts, ct), F32),
            pltpu.VMEM((1, ct), F32),
        ],
        compiler_params=_cparams("parallel", "parallel", "arbitrary"),
        name="rnn_branch",
    )(xr, cw, cb, wa, wx, ba, bx, lam, gg)


CONV_HALO = 32
CONV_LANE_CHUNK = 256
CONV_ROW_BLOCK = 32


def _conf_kernel(c_ref, dw_ref, db_ref, g_ref, b_ref, o_ref, ext_ref, y_ref):
    ts, ch = c_ref.shape
    kw = dw_ref.shape[0]
    s_idx = pl.program_id(1)

    @pl.when(s_idx == 0)
    def _():
        ext_ref[0:CONV_HALO, :] = jnp.zeros((CONV_HALO, ch), F32)

    ext_ref[CONV_HALO:CONV_HALO + ts, :] = c_ref[...]
    off0 = CONV_HALO - (kw - 1)
    rb_rows = min(CONV_ROW_BLOCK, ts)
    win_rows = CONV_HALO + rb_rows
    lc = min(CONV_LANE_CHUNK, ch)
    n_lane = ch // lc

    def body(it, carry):
        rb = it // n_lane
        lb = it % n_lane
        r0 = pl.multiple_of(rb * rb_rows, rb_rows)
        l0 = pl.multiple_of(lb * lc, lc)
        win = ext_ref[pl.ds(r0, win_rows), pl.ds(l0, lc)]
        acc = jnp.zeros((rb_rows, lc), F32) + db_ref[:, pl.ds(l0, lc)]
        for r in range(SUBLANES):
            z = win if r == 0 else pltpu.roll(win, win_rows - r, 0)
            for q in range(win_rows // SUBLANES):
                k = q * SUBLANES + r - off0
                if 0 <= k < kw:
                    acc = acc + dw_ref[k:k + 1, pl.ds(l0, lc)] * z[q * SUBLANES:q * SUBLANES + rb_rows, :]
        y_ref[pl.ds(r0, rb_rows), pl.ds(l0, lc)] = acc
        return carry

    lax.fori_loop(0, (ts // rb_rows) * n_lane, body, 0)
    ext_ref[0:CONV_HALO, :] = ext_ref[ts:ts + CONV_HALO, :]
    yn = _layer_norm_rows(y_ref[...], g_ref[...], b_ref[...])
    o_ref[...] = (yn * jax.nn.sigmoid(yn)).astype(o_ref.dtype)


def _conf_branch(c, dw, db, g, b, ts):
    bsz, seq, ch = c.shape
    kw = dw.shape[0]
    assert kw - 1 <= CONV_HALO
    vec = pl.BlockSpec((1, ch), lambda bi, s: (0, 0))
    return pl.pallas_call(
        _conf_kernel,
        grid=(bsz, seq // ts),
        in_specs=[
            pl.BlockSpec((None, ts, ch), lambda bi, s: (bi, s, 0)),
            pl.BlockSpec((kw, ch), lambda bi, s: (0, 0)),
            vec, vec, vec,
        ],
        out_specs=pl.BlockSpec((None, ts, ch), lambda bi, s: (bi, s, 0)),
        out_shape=jax.ShapeDtypeStruct((bsz, seq, ch), BF16),
        scratch_shapes=[
            pltpu.VMEM((ts + CONV_HALO, ch), F32),
            pltpu.VMEM((ts, ch), F32),
        ],
        compiler_params=_cparams("parallel", "arbitrary"),
        name="conf_branch",
    )(c, dw, db, g, b)


def _merge_kernel(hr_ref, c_ref, wr_ref, wc_ref, bc_ref, gr_ref, gc_ref, o_ref):
    yr = _dot(hr_ref[...], wr_ref[...])
    yc = _dot(c_ref[...], wc_ref[...]) + bc_ref[...]
    o_ref[...] = (gr_ref[...].astype(F32) * yr + gc_ref[...].astype(F32) * yc).astype(o_ref.dtype)


def _merge(hr, cact, wr, wc, bc, gates, d_model, tm, tn):
    m, k = hr.shape
    goff = d_model // tn
    return pl.pallas_call(
        _merge_kernel,
        grid=(d_model // tn, m // tm),
        in_specs=[
            pl.BlockSpec((tm, k), lambda j, i: (i, 0)),
            pl.BlockSpec((tm, k), lambda j, i: (i, 0)),
            pl.BlockSpec((k, tn), lambda j, i: (0, j)),
            pl.BlockSpec((k, tn), lambda j, i: (0, j)),
            pl.BlockSpec((1, tn), lambda j, i: (0, j)),
            pl.BlockSpec((tm, tn), lambda j, i: (i, j)),
            pl.BlockSpec((tm, tn), lambda j, i: (i, j + goff)),
        ],
        out_specs=pl.BlockSpec((tm, tn), lambda j, i: (i, j)),
        out_shape=jax.ShapeDtypeStruct((m, d_model), BF16),
        compiler_params=_cparams("parallel", "parallel"),
        name="merge",
    )(hr, cact, wr, wc, bc, gates, gates)


def _mm_ln_kernel(a_ref, w_ref, res_ref, g_ref, b_ref, o_ref, obf_ref, *, alpha):
    y = _dot(a_ref[...], w_ref[...])
    z = _layer_norm_rows(alpha * res_ref[...] + y, g_ref[...], b_ref[...])
    o_ref[...] = z
    obf_ref[...] = z.astype(BF16)


def _mm_ln(a, w, res, g, b, alpha, tm):
    m, k = a.shape
    n = w.shape[1]
    vec = pl.BlockSpec((1, n), lambda i: (0, 0))
    return pl.pallas_call(
        functools.partial(_mm_ln_kernel, alpha=alpha),
        grid=(m // tm,),
        in_specs=[
            pl.BlockSpec((tm, k), lambda i: (i, 0)),
            pl.BlockSpec((k, n), lambda i: (0, 0)),
            pl.BlockSpec((tm, n), lambda i: (i, 0)),
            vec, vec,
        ],
        out_specs=[pl.BlockSpec((tm, n), lambda i: (i, 0)), pl.BlockSpec((tm, n), lambda i: (i, 0))],
        out_shape=[jax.ShapeDtypeStruct((m, n), F32), jax.ShapeDtypeStruct((m, n), BF16)],
        compiler_params=_cparams("parallel"),
        name="mm_ln",
    )(a, w, res, g, b)


def _add_ln_kernel(y_ref, res_ref, g_ref, b_ref, o_ref, *, alpha):
    o_ref[...] = _layer_norm_rows(alpha * res_ref[...] + y_ref[...], g_ref[...], b_ref[...])


def _add_ln(y, res, g, b, alpha, tm):
    m, n = y.shape
    vec = pl.BlockSpec((1, n), lambda i: (0, 0))
    return pl.pallas_call(
        functools.partial(_add_ln_kernel, alpha=alpha),
        grid=(m // tm,),
        in_specs=[pl.BlockSpec((tm, n), lambda i: (i, 0)), pl.BlockSpec((tm, n), lambda i: (i, 0)), vec, vec],
        out_specs=pl.BlockSpec((tm, n), lambda i: (i, 0)),
        out_shape=jax.ShapeDtypeStruct((m, n), F32),
        compiler_params=_cparams("parallel"),
        name="add_ln",
    )(y, res, g, b)


def _attn_kernel(q_ref, k_ref, v_ref, o_ref, *, heads):
    tq, d = q_ref.shape
    hd = d // heads
    scale = hd ** -0.5
    outs = []
    for h in range(heads):
        sl = slice(h * hd, (h + 1) * hd)
        s = lax.dot_general(q_ref[:, sl], k_ref[:, sl], (((1,), (1,)), ((), ())),
                            preferred_element_type=F32) * scale
        m = jnp.max(s, axis=-1, keepdims=True)
        p = jnp.exp(s - m)
        p = p / jnp.sum(p, axis=-1, keepdims=True)
        outs.append(_dot(p.astype(BF16), v_ref[:, sl]))
    o_ref[...] = jnp.concatenate(outs, axis=1).astype(o_ref.dtype)


def _attention(q, kv, bsz, seq, mem_len, d_model, tq):
    nq = seq // tq
    return pl.pallas_call(
        functools.partial(_attn_kernel, heads=XA_HEADS),
        grid=(bsz, nq),
        in_specs=[
            pl.BlockSpec((tq, d_model), lambda b, i: (b * nq + i, 0)),
            pl.BlockSpec((mem_len, d_model), lambda b, i: (b, 0)),
            pl.BlockSpec((mem_len, d_model), lambda b, i: (b, 1)),
        ],
        out_specs=pl.BlockSpec((tq, d_model), lambda b, i: (b * nq + i, 0)),
        out_shape=jax.ShapeDtypeStruct((bsz * seq, d_model), BF16),
        compiler_params=_cparams("parallel", "parallel"),
        name="xattn",
    )(q, kv, kv)


def _extract_topk(s, k, pos, exact):
    rank = jnp.full(s.shape, float(k), F32)
    vals = []
    for r in range(k):
        m = jnp.max(s, axis=0, keepdims=True)
        hit = s == m
        if exact:
            first = jnp.min(jnp.where(hit, pos, jnp.iinfo(jnp.int32).max), axis=0, keepdims=True)
            hit = pos == first
        rank = jnp.where(hit, float(r), rank)
        s = jnp.where(hit, -jnp.inf, s)
        vals.append(m)
    n_ranked = jnp.sum(jnp.where(rank < float(k), 1.0, 0.0), axis=0, keepdims=True)
    return vals, rank, n_ranked == float(k)


def _candidate_blocks(k):
    blocks = [("row", 0, k), ("row", 1, k // 2)]
    b = 0
    while 3 * (b + 1) <= k:
        hi = k // (b + 1)
        blocks.append(("col", b, -(-hi // SUBLANES) * SUBLANES, 2, hi))
        b += 1
    return blocks


def _choose_pairs(v1, v2, rank1, k, exact):
    tt = rank1.shape[1]
    v1_all = jnp.concatenate(v1, axis=0)
    v2_all = jnp.concatenate(v2, axis=0)
    e1_all = jnp.exp(v1_all - v1[0])
    e2_all = jnp.exp(v2_all - v2[0])
    cands, prods, poss = [], [], []
    for blk in _candidate_blocks(k):
        if blk[0] == "row":
            _, a, n = blk
            cands.append(v1[a] + v2_all[0:n])
            prods.append(e1_all[a:a + 1] * e2_all[0:n])
            poss.append(a * k + lax.broadcasted_iota(jnp.int32, (n, tt), 0))
        else:
            _, b, n, lo, hi = blk
            a_col = lax.broadcasted_iota(jnp.int32, (n, tt), 0)
            valid = jnp.logical_and(a_col >= lo, a_col < hi)
            cands.append(jnp.where(valid, v1_all[0:n] + v2[b], -jnp.inf))
            prods.append(e1_all[0:n] * e2_all[b:b + 1])
            poss.append(jnp.where(valid, a_col * k + b, jnp.iinfo(jnp.int32).max))
    _, rankc, okc = _extract_topk(jnp.concatenate(cands, axis=0), k, jnp.concatenate(poss, axis=0), exact)
    sel = jnp.where(rankc < float(k), 1.0, 0.0)
    z = jnp.sum(sel * jnp.concatenate(prods, axis=0), axis=0, keepdims=True)
    cnt = jnp.zeros((k, tt), F32)
    a_col = lax.broadcasted_iota(jnp.int32, (k, tt), 0)
    row0 = 0
    for blk in _candidate_blocks(k):
        n = blk[2]
        part = sel[row0:row0 + n]
        row0 += n
        if blk[0] == "row":
            cnt = cnt + jnp.where(a_col == blk[1], jnp.sum(part, axis=0, keepdims=True), 0.0)
        elif n == k:
            cnt = cnt + part
        else:
            cnt = cnt + jnp.concatenate([part, jnp.zeros((k - n, part.shape[1]), F32)], axis=0)
    c1 = jnp.zeros_like(rank1)
    for a in range(k):
        c1 = jnp.where(rank1 == float(a), cnt[a:a + 1], c1)
    return c1, z, okc


def _route_head(s1, s2, k, exact):
    key_pos = lax.broadcasted_iota(jnp.int32, s1.shape, 0)
    v1, rank1, ok1 = _extract_topk(s1, k, key_pos, exact)
    v2, rank2, ok2 = _extract_topk(s2, k, key_pos, exact)
    c1, z, okc = _choose_pairs(v1, v2, rank1, k, exact)
    ok = jnp.logical_and(jnp.logical_and(ok1, ok2), okc)
    return c1, jnp.exp(s1 - v1[0]), rank2, jnp.exp(s2 - v2[0]) / z, ok


def _peer_route_kernel(x_ref, wq_ref, keys_ref, c1_ref, e1_ref, r2_ref, e2_ref, *, heads, topk):
    n_keys, half = keys_ref.shape[1], keys_ref.shape[2]
    q = _dot(x_ref[...], wq_ref[...])
    nt = (((1,), (1,)), ((), ()))
    count_false = lambda ok: jnp.sum(jnp.where(ok, 0.0, 1.0))

    for h in range(heads):
        base = h * 2 * half
        s1 = lax.dot_general(keys_ref[0], q[:, base:base + half], nt, preferred_element_type=F32)
        s2 = lax.dot_general(keys_ref[1], q[:, base + half:base + 2 * half], nt, preferred_element_type=F32)

        key_pos = lax.broadcasted_iota(jnp.int32, s1.shape, 0)
        v1, rank1, ok1 = _extract_topk(s1, topk, key_pos, False)
        v2, rank2, ok2 = _extract_topk(s2, topk, key_pos, False)

        def write_pairs(c1, z, s2=s2, v2=v2, h=h):
            c1_ref[h] = _dup_bf16_words(c1)
            e2_ref[h] = (jnp.exp(s2 - v2[0]) / z).astype(BF16)

        c1, z, okc = _choose_pairs(v1, v2, rank1, topk, False)
        write_pairs(c1, z)
        e1_ref[h] = _dup_bf16_words(jnp.exp(s1 - v1[0]))
        r2_ref[h] = rank2.astype(BF16)
        bad_keys = count_false(jnp.logical_and(ok1, ok2))
        bad_pairs = count_false(okc)

        @pl.when(bad_keys > 0.0)
        def _(s1=s1, s2=s2, h=h):
            c1x, e1x, rank2x, e2x, _ = _route_head(s1, s2, topk, True)
            c1_ref[h] = _dup_bf16_words(c1x)
            e1_ref[h] = _dup_bf16_words(e1x)
            r2_ref[h] = rank2x.astype(BF16)
            e2_ref[h] = e2x.astype(BF16)

        @pl.when(jnp.logical_and(bad_keys == 0.0, bad_pairs > 0.0))
        def _(v1=v1, v2=v2, rank1=rank1, write_pairs=write_pairs):
            c1x, zx, _ = _choose_pairs(v1, v2, rank1, topk, True)
            write_pairs(c1x, zx)


def _dup_bf16_words(v):
    b = lax.bitcast_convert_type(v.astype(BF16).astype(F32), jnp.uint32)
    return b | (b >> 16)


def _peer_route(x_bf, wq, keys, heads, tt):
    t, d = x_bf.shape
    n_keys = keys.shape[1]
    out_w = jax.ShapeDtypeStruct((heads, n_keys, t), jnp.uint32)
    out_b = jax.ShapeDtypeStruct((heads, n_keys, t), BF16)
    ospec = pl.BlockSpec((heads, n_keys, tt), lambda i: (0, 0, i))
    return pl.pallas_call(
        functools.partial(_peer_route_kernel, heads=heads, topk=PEER_TOPK),
        grid=(t // tt,),
        in_specs=[
            pl.BlockSpec((tt, d), lambda i: (i, 0)),
            pl.BlockSpec(wq.shape, lambda i: (0, 0)),
            pl.BlockSpec(keys.shape, lambda i: (0, 0, 0)),
        ],
        out_specs=[ospec, ospec, ospec, ospec],
        out_shape=[out_w, out_w, out_b, out_b],
        compiler_params=_cparams("parallel"),
        name="peer_route",
    )(x_bf, wq, keys)


def _peer_dense_kernel(xn_ref, un_ref, vt_ref, c1_ref, e1_ref, r2_ref, e2_ref, o_ref, acc_ref, ht_ref, act_ref, *,
                       heads, n_e):
    s_idx = pl.program_id(0)
    e_gate = jnp.maximum(s_idx - 1, 0) % n_e
    e_down = jnp.maximum(s_idx - 2, 0) % n_e
    ec = un_ref.shape[0]
    tt = xn_ref.shape[0]
    n_keys = r2_ref.shape[1]
    rows_per_step = ec // n_keys

    @pl.when(s_idx == 0)
    def _():
        ht_ref[...] = jnp.zeros_like(ht_ref)
        act_ref[...] = jnp.zeros_like(act_ref)

    @pl.when(e_down == 0)
    def _():
        acc_ref[...] = jnp.zeros_like(acc_ref)

    acc_ref[...] += _dot(vt_ref[...], act_ref[...])
    ht = ht_ref[...]
    ht_ref[...] = lax.dot_general(un_ref[...], xn_ref[...], (((1,), (1,)), ((), ())), preferred_element_type=F32)

    def row_bf16(ref, h, i):
        words = jnp.broadcast_to(ref[h, pl.ds(i, 1), :], (n_keys // 2, tt))
        return pltpu.bitcast(words, BF16)

    for ii in range(rows_per_step):
        i = e_gate * rows_per_step + ii
        g = None
        for h in range(heads):
            sel = jnp.where(r2_ref[h] < row_bf16(c1_ref, h, i), e2_ref[h], jnp.zeros((), BF16))
            term = sel * row_bf16(e1_ref, h, i)
            g = term if g is None else g + term
        rows = slice(ii * n_keys, (ii + 1) * n_keys)
        act_ref[rows, :] = jax.nn.gelu(ht[rows, :]).astype(BF16) * g

    @pl.when(jnp.logical_and(e_down == n_e - 1, s_idx > 1))
    def _():
        o_ref[...] = acc_ref[...].T


def _peer_dense(x_bf, u_bf, vt_bf, c1, e1, r2, e2, tt, ec):
    t, d = x_bf.shape
    n_exp = u_bf.shape[0]
    heads, n_keys, _ = c1.shape
    n_t, n_e = t // tt, n_exp // ec
    n_chunks = n_t * n_e
    up = lambda s: jnp.minimum(s, n_chunks - 1)
    gate = lambda s: jnp.clip(s - 1, 0, n_chunks - 1)
    down = lambda s: jnp.maximum(s - 2, 0)
    rspec = pl.BlockSpec((heads, n_keys, tt), lambda s: (0, 0, gate(s) // n_e))
    return pl.pallas_call(
        functools.partial(_peer_dense_kernel, heads=heads, n_e=n_e),
        grid=(n_chunks + 2,),
        in_specs=[
            pl.BlockSpec((tt, d), lambda s: (up(s) // n_e, 0)),
            pl.BlockSpec((ec, d), lambda s: (up(s) % n_e, 0)),
            pl.BlockSpec((None, d, ec), lambda s: (down(s) % n_e, 0, 0)),
            rspec, rspec, rspec, rspec,
        ],
        out_specs=pl.BlockSpec((tt, d), lambda s: (down(s) // n_e, 0)),
        out_shape=jax.ShapeDtypeStruct((t, d), F32),
        scratch_shapes=[pltpu.VMEM((d, tt), F32), pltpu.VMEM((ec, tt), F32), pltpu.VMEM((ec, tt), BF16)],
        compiler_params=_cparams("arbitrary"),
        name="peer_dense",
    )(x_bf, u_bf, vt_bf, c1, e1, r2, e2)


def _transpose_cast_kernel(v_ref, o_ref):
    o_ref[...] = v_ref[...].T.astype(o_ref.dtype)


def _transpose_cast(v, te):
    n_exp, d = v.shape
    return pl.pallas_call(
        _transpose_cast_kernel,
        grid=(n_exp // te,),
        in_specs=[pl.BlockSpec((te, d), lambda e: (e, 0))],
        out_specs=pl.BlockSpec((None, d, te), lambda e: (e, 0, 0)),
        out_shape=jax.ShapeDtypeStruct((n_exp // te, d, te), BF16),
        compiler_params=_cparams("parallel"),
        name="transpose_cast",
    )(v)


def _tile(n, pref):
    t = min(n, pref)
    assert n % t == 0
    return t


def _layer(x, mem, w_in, b_in, rnn_conv_w, rnn_conv_b, rnn_w_a, rnn_b_a, rnn_w_x, rnn_b_x, rnn_lambda,
           w_rnn_out, conf_dw_w, conf_dw_b, conf_ln_g, conf_ln_b, w_conf_out, b_conf_out, w_mix_out,
           ln1_g, ln1_b, xa_w_q, xa_w_k, xa_w_v, xa_w_o, ln2_g, ln2_b,
           peer_w_q, peer_sub_keys, peer_u, peer_v, ln3_g, ln3_b, alpha):
    bsz, seq, d = x.shape
    t = bsz * seq
    mem_len = mem.shape[1]
    d_rnn = rnn_conv_w.shape[1]
    d_conv = conf_dw_w.shape[1]
    row = lambda v: v.reshape(1, -1)

    tm = _tile(t, 1024)
    tn = _tile(d, 1024)
    x2d = x.reshape(t, d)
    x_bf = x2d.astype(BF16)
    w_in_bf = w_in.astype(BF16)
    b_in2 = row(b_in)

    xr = _proj(x_bf, w_in_bf, b_in2, 0, d_rnn, "none", F32, tm, tn)
    gg = _proj(x_bf, w_in_bf, b_in2, d_rnn, d_rnn, "gelu", BF16, tm, tn)
    cglu = _glu(x_bf, w_in_bf, b_in2, 2 * d_rnn, 2 * d_rnn + d_conv, d_conv, tm, tn)
    mgates = _proj(x_bf, w_in_bf, b_in2, 2 * d_rnn + 2 * d_conv, 2 * d, "sigmoid", BF16, tm, tn)

    hr = _rnn_branch(xr.reshape(bsz, seq, d_rnn), rnn_conv_w, row(rnn_conv_b),
                     rnn_w_a.astype(BF16), rnn_w_x.astype(BF16), row(rnn_b_a), row(rnn_b_x),
                     row(rnn_lambda), gg.reshape(bsz, seq, d_rnn),
                     _tile(seq, 512), _tile(d_rnn, 512))
    cact = _conf_branch(cglu.reshape(bsz, seq, d_conv), conf_dw_w, row(conf_dw_b),
                        row(conf_ln_g), row(conf_ln_b), _tile(seq, 256))

    merged = _merge(hr.reshape(t, d_rnn), cact.reshape(t, d_conv), w_rnn_out.astype(BF16),
                    w_conf_out.astype(BF16), row(b_conf_out), mgates, d, tm, tn)
    tl = _tile(t, 512)
    x1, x1_bf = _mm_ln(merged, w_mix_out.astype(BF16), x2d, row(ln1_g), row(ln1_b), alpha, tl)

    zero_d = jnp.zeros((1, d), F32)
    q = _proj(x1_bf, xa_w_q.astype(BF16), zero_d, 0, d, "none", BF16, tm, tn)
    w_kv = jnp.concatenate([xa_w_k, xa_w_v], axis=1).astype(BF16)
    mem_bf = mem.reshape(bsz * mem_len, d).astype(BF16)
    kv = _proj(mem_bf, w_kv, jnp.zeros((1, 2 * d), F32), 0, 2 * d, "none", BF16,
               _tile(bsz * mem_len, 1024), tn)
    o = _attention(q, kv, bsz, seq, mem_len, d, _tile(seq, 512))
    x2, x2_bf = _mm_ln(o, xa_w_o.astype(BF16), x1, row(ln2_g), row(ln2_b), alpha, tl)

    heads = peer_w_q.shape[1] // (2 * peer_sub_keys.shape[2])
    c1, e1, r2, e2 = _peer_route(x2_bf, peer_w_q.astype(BF16), peer_sub_keys, heads, _tile(t, 256))
    n_exp = peer_u.shape[0]
    ec = _tile(n_exp, 1024)
    vt_bf = _transpose_cast(peer_v, ec)
    ff = _peer_dense(x2_bf, peer_u.astype(BF16), vt_bf, c1, e1, r2, e2, _tile(t, 512), ec)
    x3 = _add_ln(ff, x2, row(ln3_g), row(ln3_b), alpha, tl)
    return x3.reshape(bsz, seq, d)


def kernel(x, mem, w_in, b_in, rnn_conv_w, rnn_conv_b, rnn_w_a, rnn_b_a, rnn_w_x, rnn_b_x, rnn_lambda, w_rnn_out, conf_dw_w, conf_dw_b, conf_ln_g, conf_ln_b, w_conf_out, b_conf_out, w_mix_out, ln1_g, ln1_b, xa_w_q, xa_w_k, xa_w_v, xa_w_o, ln2_g, ln2_b, peer_w_q, peer_sub_keys, peer_u, peer_v, ln3_g, ln3_b):
    depth = w_in.shape[0]
    alpha = (2 * depth) ** 0.25
    params = (w_in, b_in, rnn_conv_w, rnn_conv_b, rnn_w_a, rnn_b_a, rnn_w_x, rnn_b_x, rnn_lambda, w_rnn_out,
              conf_dw_w, conf_dw_b, conf_ln_g, conf_ln_b, w_conf_out, b_conf_out, w_mix_out, ln1_g, ln1_b,
              xa_w_q, xa_w_k, xa_w_v, xa_w_o, ln2_g, ln2_b, peer_w_q, peer_sub_keys, peer_u, peer_v,
              ln3_g, ln3_b)
    for l in range(depth):
        x = _layer(x, mem, *[p[l] for p in params], alpha)
    return x
```

```python
import functools
import math

import jax
import jax.numpy as jnp
from jax import lax
from jax.experimental import pallas as pl
from jax.experimental.pallas import tpu as pltpu

F32 = jnp.float32
BF16 = jnp.bfloat16

LRU_C = 8.0
LN_EPS = 1e-5
XA_HEADS = 4
PEER_TOPK = 16
VMEM_LIMIT_BYTES = 56 * 1024 * 1024
LANES = 128
SUBLANES = 8


def _cparams(*sem):
    return pltpu.CompilerParams(dimension_semantics=sem, vmem_limit_bytes=VMEM_LIMIT_BYTES)


def _layer_norm_rows(z, g, b):
    mu = jnp.mean(z, axis=-1, keepdims=True)
    zc = z - mu
    var = jnp.mean(zc * zc, axis=-1, keepdims=True)
    return zc * lax.rsqrt(var + LN_EPS) * g + b


def _dot(a, b):
    return jnp.dot(a, b, preferred_element_type=F32)


def _cast_weights_once(step, pairs):
    @pl.when(step == 0)
    def _():
        for w_ref, wbf_ref in pairs:
            wbf_ref[...] = w_ref[...].astype(BF16)


def _proj_kernel(a_ref, w_ref, b_ref, o_ref, wbf_ref, *, act):
    _cast_weights_once(pl.program_id(1), [(w_ref, wbf_ref)])
    y = _dot(a_ref[...], wbf_ref[...]) + b_ref[...]
    if act == "gelu":
        y = jax.nn.gelu(y)
    elif act == "sigmoid":
        y = jax.nn.sigmoid(y)
    o_ref[...] = y.astype(o_ref.dtype)


def _proj(a, w, b, col0, ncols, act, out_dtype, tm, tn):
    m, k = a.shape
    c0 = col0 // tn
    return pl.pallas_call(
        functools.partial(_proj_kernel, act=act),
        grid=(ncols // tn, m // tm),
        in_specs=[
            pl.BlockSpec((tm, k), lambda j, i: (i, 0)),
            pl.BlockSpec((k, tn), lambda j, i: (0, j + c0)),
            pl.BlockSpec((1, tn), lambda j, i: (0, j + c0)),
        ],
        out_specs=pl.BlockSpec((tm, tn), lambda j, i: (i, j)),
        out_shape=jax.ShapeDtypeStruct((m, ncols), out_dtype),
        scratch_shapes=[pltpu.VMEM((k, tn), BF16)],
        compiler_params=_cparams("parallel", "arbitrary"),
        name="proj_" + act,
    )(a, w, b)


def _glu_kernel(a_ref, w1_ref, w2_ref, b1_ref, b2_ref, o_ref, w1bf_ref, w2bf_ref):
    _cast_weights_once(pl.program_id(1), [(w1_ref, w1bf_ref), (w2_ref, w2bf_ref)])
    a = a_ref[...]
    y1 = _dot(a, w1bf_ref[...]) + b1_ref[...]
    y2 = _dot(a, w2bf_ref[...]) + b2_ref[...]
    o_ref[...] = (y1 * jax.nn.sigmoid(y2)).astype(o_ref.dtype)


def _glu(a, w, b, col1, col2, ncols, tm, tn):
    m, k = a.shape
    c1, c2 = col1 // tn, col2 // tn
    return pl.pallas_call(
        _glu_kernel,
        grid=(ncols // tn, m // tm),
        in_specs=[
            pl.BlockSpec((tm, k), lambda j, i: (i, 0)),
            pl.BlockSpec((k, tn), lambda j, i: (0, j + c1)),
            pl.BlockSpec((k, tn), lambda j, i: (0, j + c2)),
            pl.BlockSpec((1, tn), lambda j, i: (0, j + c1)),
            pl.BlockSpec((1, tn), lambda j, i: (0, j + c2)),
        ],
        out_specs=pl.BlockSpec((tm, tn), lambda j, i: (i, j)),
        out_shape=jax.ShapeDtypeStruct((m, ncols), F32),
        scratch_shapes=[pltpu.VMEM((k, tn), BF16), pltpu.VMEM((k, tn), BF16)],
        compiler_params=_cparams("parallel", "arbitrary"),
        name="proj_glu",
    )(a, w, w, b, b)


def _rnn_kernel(xr_ref, cw_ref, cb_ref, wa_ref, wx_ref, ba_ref, bx_ref, lam_ref, gg_ref, o_ref,
                ext_ref, a_ref, u_ref, h_ref):
    ts, ct = xr_ref.shape
    kw = cw_ref.shape[0]
    s_idx = pl.program_id(2)

    @pl.when(s_idx == 0)
    def _():
        ext_ref[0:SUBLANES, :] = jnp.zeros((SUBLANES, ct), F32)
        h_ref[...] = jnp.zeros_like(h_ref)

    ext_ref[SUBLANES:SUBLANES + ts, :] = xr_ref[...]
    y = jnp.zeros((ts, ct), F32) + cb_ref[...]
    ext = ext_ref[...]
    for k in range(kw):
        off = SUBLANES - (kw - 1) + k
        z = ext if off % SUBLANES == 0 else pltpu.roll(ext, ts + SUBLANES - off % SUBLANES, 0)
        base = off - off % SUBLANES
        y = y + cw_ref[k:k + 1, :] * z[base:base + ts, :]
    ext_ref[0:SUBLANES, :] = ext_ref[ts:ts + SUBLANES, :]

    nblk = ct // LANES
    r_parts, i_parts = [], []
    for blk in range(nblk):
        yb = y[:, blk * LANES:(blk + 1) * LANES].astype(BF16)
        r_parts.append(_dot(yb, wa_ref[blk]))
        i_parts.append(_dot(yb, wx_ref[blk]))
    r = jax.nn.sigmoid(jnp.concatenate(r_parts, axis=1) + ba_ref[...])
    ig = jax.nn.sigmoid(jnp.concatenate(i_parts, axis=1) + bx_ref[...])
    lam = lam_ref[...]
    sp = jnp.maximum(-lam, 0.0) + jnp.log1p(jnp.exp(-jnp.abs(lam)))
    log_a = (-LRU_C * r) * sp
    a = jnp.exp(log_a)
    u = jnp.sqrt(1.0 - jnp.exp(2.0 * log_a)) * (ig * y)
    a_ref[...] = a
    u_ref[...] = u

    row = lax.broadcasted_iota(jnp.int32, (SUBLANES, ct), 0)

    def body(g, h):
        r0 = pl.multiple_of(g * SUBLANES, SUBLANES)
        av = a_ref[pl.ds(r0, SUBLANES), :]
        uv = u_ref[pl.ds(r0, SUBLANES), :]
        for d in (1, 2, 4):
            a_sh = jnp.where(row >= d, pltpu.roll(av, d, 0), 1.0)
            u_sh = jnp.where(row >= d, pltpu.roll(uv, d, 0), 0.0)
            uv = av * u_sh + uv
            av = av * a_sh
        hh = uv + av * h
        u_ref[pl.ds(r0, SUBLANES), :] = hh
        return hh[SUBLANES - 1:SUBLANES, :]

    h_last = lax.fori_loop(0, ts // SUBLANES, body, h_ref[...], unroll=4)
    h_ref[...] = h_last
    o_ref[...] = (u_ref[...] * gg_ref[...].astype(F32)).astype(o_ref.dtype)


def _rnn_branch(xr, cw, cb, wa, wx, ba, bx, lam, gg, ts, ct):
    bsz, seq, c = xr.shape
    kw = cw.shape[0]
    assert kw - 1 <= SUBLANES
    nb = ct // LANES
    vec = pl.BlockSpec((1, ct), lambda b, j, s: (0, j))
    return pl.pallas_call(
        _rnn_kernel,
        grid=(bsz, c // ct, seq // ts),
        in_specs=[
            pl.BlockSpec((None, ts, ct), lambda b, j, s: (b, s, j)),
            pl.BlockSpec((kw, ct), lambda b, j, s: (0, j)),
            vec,
            pl.BlockSpec((nb, LANES, LANES), lambda b, j, s: (j, 0, 0)),
            pl.BlockSpec((nb, LANES, LANES), lambda b, j, s: (j, 0, 0)),
            vec, vec, vec,
            pl.BlockSpec((None, ts, ct), lambda b, j, s: (b, s, j)),
        ],
        out_specs=pl.BlockSpec((None, ts, ct), lambda b, j, s: (b, s, j)),
        out_shape=jax.ShapeDtypeStruct((bsz, seq, c), BF16),
        scratch_shapes=[
            pltpu.VMEM((ts + SUBLANES, ct), F32),
            pltpu.VMEM((ts, ct), F32),
            pltpu.VMEM((ts, ct), F32),
            pltpu.VMEM((1, ct), F32),
        ],
        compiler_params=_cparams("parallel", "parallel", "arbitrary"),
        name="rnn_branch",
    )(xr, cw, cb, wa, wx, ba, bx, lam, gg)


CONV_HALO = 32
CONV_LANE_CHUNK = 256
CONV_ROW_BLOCK = 32


def _conf_kernel(c_ref, dw_ref, db_ref, g_ref, b_ref, o_ref, ext_ref, y_ref):
    ts, ch = c_ref.shape
    kw = dw_ref.shape[0]
    s_idx = pl.program_id(1)

    @pl.when(s_idx == 0)
    def _():
        ext_ref[0:CONV_HALO, :] = jnp.zeros((CONV_HALO, ch), F32)

    ext_ref[CONV_HALO:CONV_HALO + ts, :] = c_ref[...]
    off0 = CONV_HALO - (kw - 1)
    rb_rows = min(CONV_ROW_BLOCK, ts)
    win_rows = CONV_HALO + rb_rows
    lc = min(CONV_LANE_CHUNK, ch)
    n_lane = ch // lc

    def body(it, carry):
        rb = it // n_lane
        lb = it % n_lane
        r0 = pl.multiple_of(rb * rb_rows, rb_rows)
        l0 = pl.multiple_of(lb * lc, lc)
        win = ext_ref[pl.ds(r0, win_rows), pl.ds(l0, lc)]
        acc = jnp.zeros((rb_rows, lc), F32) + db_ref[:, pl.ds(l0, lc)]
        for r in range(SUBLANES):
            z = win if r == 0 else pltpu.roll(win, win_rows - r, 0)
            for q in range(win_rows // SUBLANES):
                k = q * SUBLANES + r - off0
                if 0 <= k < kw:
                    acc = acc + dw_ref[k:k + 1, pl.ds(l0, lc)] * z[q * SUBLANES:q * SUBLANES + rb_rows, :]
        y_ref[pl.ds(r0, rb_rows), pl.ds(l0, lc)] = acc
        return carry

    lax.fori_loop(0, (ts // rb_rows) * n_lane, body, 0)
    ext_ref[0:CONV_HALO, :] = ext_ref[ts:ts + CONV_HALO, :]
    yn = _layer_norm_rows(y_ref[...], g_ref[...], b_ref[...])
    o_ref[...] = (yn * jax.nn.sigmoid(yn)).astype(o_ref.dtype)


def _conf_branch(c, dw, db, g, b, ts):
    bsz, seq, ch = c.shape
    kw = dw.shape[0]
    assert kw - 1 <= CONV_HALO
    vec = pl.BlockSpec((1, ch), lambda bi, s: (0, 0))
    return pl.pallas_call(
        _conf_kernel,
        grid=(bsz, seq // ts),
        in_specs=[
            pl.BlockSpec((None, ts, ch), lambda bi, s: (bi, s, 0)),
            pl.BlockSpec((kw, ch), lambda bi, s: (0, 0)),
            vec, vec, vec,
        ],
        out_specs=pl.BlockSpec((None, ts, ch), lambda bi, s: (bi, s, 0)),
        out_shape=jax.ShapeDtypeStruct((bsz, seq, ch), BF16),
        scratch_shapes=[
            pltpu.VMEM((ts + CONV_HALO, ch), F32),
            pltpu.VMEM((ts, ch), F32),
        ],
        compiler_params=_cparams("parallel", "arbitrary"),
        name="conf_branch",
    )(c, dw, db, g, b)


def _merge_kernel(hr_ref, c_ref, wr_ref, wc_ref, bc_ref, gr_ref, gc_ref, o_ref, wrbf_ref, wcbf_ref):
    _cast_weights_once(pl.program_id(1), [(wr_ref, wrbf_ref), (wc_ref, wcbf_ref)])
    yr = _dot(hr_ref[...], wrbf_ref[...])
    yc = _dot(c_ref[...], wcbf_ref[...]) + bc_ref[...]
    o_ref[...] = (gr_ref[...].astype(F32) * yr + gc_ref[...].astype(F32) * yc).astype(o_ref.dtype)


def _merge(hr, cact, wr, wc, bc, gates, d_model, tm, tn):
    m, k = hr.shape
    goff = d_model // tn
    return pl.pallas_call(
        _merge_kernel,
        grid=(d_model // tn, m // tm),
        in_specs=[
            pl.BlockSpec((tm, k), lambda j, i: (i, 0)),
            pl.BlockSpec((tm, k), lambda j, i: (i, 0)),
            pl.BlockSpec((k, tn), lambda j, i: (0, j)),
            pl.BlockSpec((k, tn), lambda j, i: (0, j)),
            pl.BlockSpec((1, tn), lambda j, i: (0, j)),
            pl.BlockSpec((tm, tn), lambda j, i: (i, j)),
            pl.BlockSpec((tm, tn), lambda j, i: (i, j + goff)),
        ],
        out_specs=pl.BlockSpec((tm, tn), lambda j, i: (i, j)),
        out_shape=jax.ShapeDtypeStruct((m, d_model), BF16),
        scratch_shapes=[pltpu.VMEM((k, tn), BF16), pltpu.VMEM((k, tn), BF16)],
        compiler_params=_cparams("parallel", "arbitrary"),
        name="merge",
    )(hr, cact, wr, wc, bc, gates, gates)


def _mm_ln_kernel(a_ref, w_ref, res_ref, g_ref, b_ref, o_ref, obf_ref, wbf_ref, *, alpha):
    _cast_weights_once(pl.program_id(0), [(w_ref, wbf_ref)])
    y = _dot(a_ref[...], wbf_ref[...])
    z = _layer_norm_rows(alpha * res_ref[...] + y, g_ref[...], b_ref[...])
    o_ref[...] = z
    obf_ref[...] = z.astype(BF16)


def _mm_ln(a, w, res, g, b, alpha, tm):
    m, k = a.shape
    n = w.shape[1]
    vec = pl.BlockSpec((1, n), lambda i: (0, 0))
    return pl.pallas_call(
        functools.partial(_mm_ln_kernel, alpha=alpha),
        grid=(m // tm,),
        in_specs=[
            pl.BlockSpec((tm, k), lambda i: (i, 0)),
            pl.BlockSpec((k, n), lambda i: (0, 0), pipeline_mode=pl.Buffered(1)),
            pl.BlockSpec((tm, n), lambda i: (i, 0)),
            vec, vec,
        ],
        out_specs=[pl.BlockSpec((tm, n), lambda i: (i, 0)), pl.BlockSpec((tm, n), lambda i: (i, 0))],
        out_shape=[jax.ShapeDtypeStruct((m, n), F32), jax.ShapeDtypeStruct((m, n), BF16)],
        scratch_shapes=[pltpu.VMEM((k, n), BF16)],
        compiler_params=_cparams("arbitrary"),
        name="mm_ln",
    )(a, w, res, g, b)


def _add_ln_kernel(y_ref, res_ref, g_ref, b_ref, o_ref, *, alpha):
    o_ref[...] = _layer_norm_rows(alpha * res_ref[...] + y_ref[...], g_ref[...], b_ref[...])


def _add_ln(y, res, g, b, alpha, tm):
    m, n = y.shape
    vec = pl.BlockSpec((1, n), lambda i: (0, 0))
    return pl.pallas_call(
        functools.partial(_add_ln_kernel, alpha=alpha),
        grid=(m // tm,),
        in_specs=[pl.BlockSpec((tm, n), lambda i: (i, 0)), pl.BlockSpec((tm, n), lambda i: (i, 0)), vec, vec],
        out_specs=pl.BlockSpec((tm, n), lambda i: (i, 0)),
        out_shape=jax.ShapeDtypeStruct((m, n), F32),
        compiler_params=_cparams("parallel"),
        name="add_ln",
    )(y, res, g, b)


def _attn_kernel(q_ref, k_ref, v_ref, o_ref, *, heads):
    tq, d = q_ref.shape
    hd = d // heads
    scale = hd ** -0.5
    outs = []
    for h in range(heads):
        sl = slice(h * hd, (h + 1) * hd)
        s = lax.dot_general(q_ref[:, sl], k_ref[:, sl], (((1,), (1,)), ((), ())),
                            preferred_element_type=F32) * scale
        m = jnp.max(s, axis=-1, keepdims=True)
        p = jnp.exp(s - m)
        p = p / jnp.sum(p, axis=-1, keepdims=True)
        outs.append(_dot(p.astype(BF16), v_ref[:, sl]))
    o_ref[...] = jnp.concatenate(outs, axis=1).astype(o_ref.dtype)


def _attention(q, k, v, bsz, seq, mem_len, d_model, tq):
    nq = seq // tq
    return pl.pallas_call(
        functools.partial(_attn_kernel, heads=XA_HEADS),
        grid=(bsz, nq),
        in_specs=[
            pl.BlockSpec((tq, d_model), lambda b, i: (b * nq + i, 0)),
            pl.BlockSpec((mem_len, d_model), lambda b, i: (b, 0)),
            pl.BlockSpec((mem_len, d_model), lambda b, i: (b, 0)),
        ],
        out_specs=pl.BlockSpec((tq, d_model), lambda b, i: (b * nq + i, 0)),
        out_shape=jax.ShapeDtypeStruct((bsz * seq, d_model), BF16),
        compiler_params=_cparams("parallel", "parallel"),
        name="xattn",
    )(q, k, v)


def _extract_topk(s, k, pos, exact):
    rank = jnp.full(s.shape, float(k), F32)
    vals = []
    for r in range(k):
        m = jnp.max(s, axis=0, keepdims=True)
        hit = s == m
        if exact:
            first = jnp.min(jnp.where(hit, pos, jnp.iinfo(jnp.int32).max), axis=0, keepdims=True)
            hit = pos == first
        rank = jnp.where(hit, float(r), rank)
        s = jnp.where(hit, -jnp.inf, s)
        vals.append(m)
    n_ranked = jnp.sum(jnp.where(rank < float(k), 1.0, 0.0), axis=0, keepdims=True)
    return vals, rank, n_ranked == float(k)


def _candidate_blocks(k):
    blocks = [("row", 0, k), ("row", 1, k // 2)]
    b = 0
    while 3 * (b + 1) <= k:
        hi = k // (b + 1)
        blocks.append(("col", b, -(-hi // SUBLANES) * SUBLANES, 2, hi))
        b += 1
    return blocks


def _choose_pairs(v1, v2, rank1, k, exact):
    tt = rank1.shape[1]
    v1_all = jnp.concatenate(v1, axis=0)
    v2_all = jnp.concatenate(v2, axis=0)
    e1_all = jnp.exp(v1_all - v1[0])
    e2_all = jnp.exp(v2_all - v2[0])
    cands, prods, poss = [], [], []
    for blk in _candidate_blocks(k):
        if blk[0] == "row":
            _, a, n = blk
            cands.append(v1[a] + v2_all[0:n])
            prods.append(e1_all[a:a + 1] * e2_all[0:n])
            poss.append(a * k + lax.broadcasted_iota(jnp.int32, (n, tt), 0))
        else:
            _, b, n, lo, hi = blk
            a_col = lax.broadcasted_iota(jnp.int32, (n, tt), 0)
            valid = jnp.logical_and(a_col >= lo, a_col < hi)
            cands.append(jnp.where(valid, v1_all[0:n] + v2[b], -jnp.inf))
            prods.append(e1_all[0:n] * e2_all[b:b + 1])
            poss.append(jnp.where(valid, a_col * k + b, jnp.iinfo(jnp.int32).max))
    _, rankc, okc = _extract_topk(jnp.concatenate(cands, axis=0), k, jnp.concatenate(poss, axis=0), exact)
    sel = jnp.where(rankc < float(k), 1.0, 0.0)
    z = jnp.sum(sel * jnp.concatenate(prods, axis=0), axis=0, keepdims=True)
    cnt = jnp.zeros((k, tt), F32)
    a_col = lax.broadcasted_iota(jnp.int32, (k, tt), 0)
    row0 = 0
    for blk in _candidate_blocks(k):
        n = blk[2]
        part = sel[row0:row0 + n]
        row0 += n
        if blk[0] == "row":
            cnt = cnt + jnp.where(a_col == blk[1], jnp.sum(part, axis=0, keepdims=True), 0.0)
        elif n == k:
            cnt = cnt + part
        else:
            cnt = cnt + jnp.concatenate([part, jnp.zeros((k - n, part.shape[1]), F32)], axis=0)
    c1 = jnp.zeros_like(rank1)
    for a in range(k):
        c1 = jnp.where(rank1 == float(a), cnt[a:a + 1], c1)
    return c1, z, okc


def _route_head(s1, s2, k, exact):
    key_pos = lax.broadcasted_iota(jnp.int32, s1.shape, 0)
    v1, rank1, ok1 = _extract_topk(s1, k, key_pos, exact)
    v2, rank2, ok2 = _extract_topk(s2, k, key_pos, exact)
    c1, z, okc = _choose_pairs(v1, v2, rank1, k, exact)
    ok = jnp.logical_and(jnp.logical_and(ok1, ok2), okc)
    return c1, jnp.exp(s1 - v1[0]), rank2, jnp.exp(s2 - v2[0]) / z, ok


def _peer_route_kernel(x_ref, wq_ref, keys_ref, c1_ref, e1_ref, r2_ref, e2_ref, wqbf_ref, *, heads, topk):
    n_keys, half = keys_ref.shape[1], keys_ref.shape[2]
    _cast_weights_once(pl.program_id(0), [(wq_ref, wqbf_ref)])
    q = _dot(x_ref[...], wqbf_ref[...])
    nt = (((1,), (1,)), ((), ()))

    def route_all(exact):
        ok = None
        for h in range(heads):
            base = h * 2 * half
            s1 = lax.dot_general(keys_ref[0], q[:, base:base + half], nt, preferred_element_type=F32)
            s2 = lax.dot_general(keys_ref[1], q[:, base + half:base + 2 * half], nt, preferred_element_type=F32)
            c1, e1, rank2, e2, ok_h = _route_head(s1, s2, topk, exact)
            c1_ref[h] = _dup_bf16_words(c1)
            e1_ref[h] = _dup_bf16_words(e1)
            r2_ref[h] = rank2.astype(BF16)
            e2_ref[h] = e2.astype(BF16)
            ok = ok_h if ok is None else jnp.logical_and(ok, ok_h)
        return ok

    ok = route_all(exact=False)
    n_bad = jnp.sum(jnp.where(ok, 0.0, 1.0))

    @pl.when(n_bad > 0.0)
    def _():
        route_all(exact=True)


def _dup_bf16_words(v):
    b = lax.bitcast_convert_type(v.astype(BF16).astype(F32), jnp.uint32)
    return b | (b >> 16)


def _peer_route(x_bf, wq, keys, heads, tt):
    t, d = x_bf.shape
    n_keys = keys.shape[1]
    out_w = jax.ShapeDtypeStruct((heads, n_keys, t), jnp.uint32)
    out_b = jax.ShapeDtypeStruct((heads, n_keys, t), BF16)
    ospec = pl.BlockSpec((heads, n_keys, tt), lambda i: (0, 0, i))
    return pl.pallas_call(
        functools.partial(_peer_route_kernel, heads=heads, topk=PEER_TOPK),
        grid=(t // tt,),
        in_specs=[
            pl.BlockSpec((tt, d), lambda i: (i, 0)),
            pl.BlockSpec(wq.shape, lambda i: (0, 0)),
            pl.BlockSpec(keys.shape, lambda i: (0, 0, 0)),
        ],
        out_specs=[ospec, ospec, ospec, ospec],
        out_shape=[out_w, out_w, out_b, out_b],
        scratch_shapes=[pltpu.VMEM(wq.shape, BF16)],
        compiler_params=_cparams("arbitrary"),
        name="peer_route",
    )(x_bf, wq, keys)


def _peer_dense_kernel(xn_ref, un_ref, vt_ref, c1_ref, e1_ref, r2_ref, e2_ref, o_ref, acc_ref, ht_ref, act_ref, *,
                       heads, n_e):
    s_idx = pl.program_id(0)
    e_gate = jnp.maximum(s_idx - 1, 0) % n_e
    e_down = jnp.maximum(s_idx - 2, 0) % n_e
    ec = un_ref.shape[0]
    tt = xn_ref.shape[0]
    n_keys = r2_ref.shape[1]
    rows_per_step = ec // n_keys

    @pl.when(s_idx == 0)
    def _():
        ht_ref[...] = jnp.zeros_like(ht_ref)
        act_ref[...] = jnp.zeros_like(act_ref)

    @pl.when(e_down == 0)
    def _():
        acc_ref[...] = jnp.zeros_like(acc_ref)

    acc_ref[...] += _dot(vt_ref[...], act_ref[...])
    ht = ht_ref[...]
    ht_ref[...] = lax.dot_general(un_ref[...], xn_ref[...], (((1,), (1,)), ((), ())), preferred_element_type=F32)

    def row_bf16(ref, h, i):
        words = jnp.broadcast_to(ref[h, pl.ds(i, 1), :], (n_keys // 2, tt))
        return pltpu.bitcast(words, BF16)

    for ii in range(rows_per_step):
        i = e_gate * rows_per_step + ii
        g = None
        for h in range(heads):
            sel = jnp.where(r2_ref[h] < row_bf16(c1_ref, h, i), e2_ref[h], jnp.zeros((), BF16))
            term = sel * row_bf16(e1_ref, h, i)
            g = term if g is None else g + term
        rows = slice(ii * n_keys, (ii + 1) * n_keys)
        act_ref[rows, :] = jax.nn.gelu(ht[rows, :]).astype(BF16) * g

    @pl.when(jnp.logical_and(e_down == n_e - 1, s_idx > 1))
    def _():
        o_ref[...] = acc_ref[...].T


def _peer_dense(x_bf, u_bf, vt_bf, c1, e1, r2, e2, tt, ec):
    t, d = x_bf.shape
    n_exp = u_bf.shape[0]
    heads, n_keys, _ = c1.shape
    n_t, n_e = t // tt, n_exp // ec
    n_chunks = n_t * n_e
    up = lambda s: jnp.minimum(s, n_chunks - 1)
    gate = lambda s: jnp.clip(s - 1, 0, n_chunks - 1)
    down = lambda s: jnp.maximum(s - 2, 0)
    rspec = pl.BlockSpec((heads, n_keys, tt), lambda s: (0, 0, gate(s) // n_e))
    return pl.pallas_call(
        functools.partial(_peer_dense_kernel, heads=heads, n_e=n_e),
        grid=(n_chunks + 2,),
        in_specs=[
            pl.BlockSpec((tt, d), lambda s: (up(s) // n_e, 0)),
            pl.BlockSpec((ec, d), lambda s: (up(s) % n_e, 0)),
            pl.BlockSpec((None, d, ec), lambda s: (down(s) % n_e, 0, 0)),
            rspec, rspec, rspec, rspec,
        ],
        out_specs=pl.BlockSpec((tt, d), lambda s: (down(s) // n_e, 0)),
        out_shape=jax.ShapeDtypeStruct((t, d), F32),
        scratch_shapes=[pltpu.VMEM((d, tt), F32), pltpu.VMEM((ec, tt), F32), pltpu.VMEM((ec, tt), BF16)],
        compiler_params=_cparams("arbitrary"),
        name="peer_dense",
    )(x_bf, u_bf, vt_bf, c1, e1, r2, e2)


def _transpose_cast_kernel(v_ref, o_ref):
    o_ref[...] = v_ref[...].T.astype(o_ref.dtype)


def _transpose_cast(v, te):
    n_exp, d = v.shape
    return pl.pallas_call(
        _transpose_cast_kernel,
        grid=(n_exp // te,),
        in_specs=[pl.BlockSpec((te, d), lambda e: (e, 0))],
        out_specs=pl.BlockSpec((None, d, te), lambda e: (e, 0, 0)),
        out_shape=jax.ShapeDtypeStruct((n_exp // te, d, te), BF16),
        compiler_params=_cparams("parallel"),
        name="transpose_cast",
    )(v)


def _tile(n, pref):
    t = min(n, pref)
    assert n % t == 0
    return t


def _layer(x, mem, w_in, b_in, rnn_conv_w, rnn_conv_b, rnn_w_a, rnn_b_a, rnn_w_x, rnn_b_x, rnn_lambda,
           w_rnn_out, conf_dw_w, conf_dw_b, conf_ln_g, conf_ln_b, w_conf_out, b_conf_out, w_mix_out,
           ln1_g, ln1_b, xa_w_q, xa_w_k, xa_w_v, xa_w_o, ln2_g, ln2_b,
           peer_w_q, peer_sub_keys, peer_u, peer_v, ln3_g, ln3_b, alpha):
    bsz, seq, d = x.shape
    t = bsz * seq
    mem_len = mem.shape[1]
    d_rnn = rnn_conv_w.shape[1]
    d_conv = conf_dw_w.shape[1]
    row = lambda v: v.reshape(1, -1)

    tm = _tile(t, 1024)
    tn = _tile(d, 1024)
    x2d = x.reshape(t, d)
    x_bf = x2d.astype(BF16)
    b_in2 = row(b_in)
    tn2 = _tile(d, 512)

    xr = _proj(x_bf, w_in, b_in2, 0, d_rnn, "none", F32, tm, tn)
    gg = _proj(x_bf, w_in, b_in2, d_rnn, d_rnn, "gelu", BF16, tm, tn)
    cglu = _glu(x_bf, w_in, b_in2, 2 * d_rnn, 2 * d_rnn + d_conv, d_conv, tm, tn2)
    mgates = _proj(x_bf, w_in, b_in2, 2 * d_rnn + 2 * d_conv, 2 * d, "sigmoid", BF16, tm, tn)

    hr = _rnn_branch(xr.reshape(bsz, seq, d_rnn), rnn_conv_w, row(rnn_conv_b),
                     rnn_w_a.astype(BF16), rnn_w_x.astype(BF16), row(rnn_b_a), row(rnn_b_x),
                     row(rnn_lambda), gg.reshape(bsz, seq, d_rnn),
                     _tile(seq, 512), _tile(d_rnn, 512))
    cact = _conf_branch(cglu.reshape(bsz, seq, d_conv), conf_dw_w, row(conf_dw_b),
                        row(conf_ln_g), row(conf_ln_b), _tile(seq, 256))

    merged = _merge(hr.reshape(t, d_rnn), cact.reshape(t, d_conv), w_rnn_out,
                    w_conf_out, row(b_conf_out), mgates, d, tm, tn2)
    tl = _tile(t, 512)
    x1, x1_bf = _mm_ln(merged, w_mix_out, x2d, row(ln1_g), row(ln1_b), alpha, tl)

    zero_d = jnp.zeros((1, d), F32)
    q = _proj(x1_bf, xa_w_q, zero_d, 0, d, "none", BF16, tm, tn)
    mem_bf = mem.reshape(bsz * mem_len, d).astype(BF16)
    tmem = _tile(bsz * mem_len, 1024)
    k = _proj(mem_bf, xa_w_k, zero_d, 0, d, "none", BF16, tmem, tn)
    v = _proj(mem_bf, xa_w_v, zero_d, 0, d, "none", BF16, tmem, tn)
    o = _attention(q, k, v, bsz, seq, mem_len, d, _tile(seq, 512))
    x2, x2_bf = _mm_ln(o, xa_w_o, x1, row(ln2_g), row(ln2_b), alpha, tl)

    heads = peer_w_q.shape[1] // (2 * peer_sub_keys.shape[2])
    c1, e1, r2, e2 = _peer_route(x2_bf, peer_w_q, peer_sub_keys, heads, _tile(t, 256))
    n_exp = peer_u.shape[0]
    ec = _tile(n_exp, 1024)
    vt_bf = _transpose_cast(peer_v, ec)
    ff = _peer_dense(x2_bf, peer_u.astype(BF16), vt_bf, c1, e1, r2, e2, _tile(t, 512), ec)
    x3 = _add_ln(ff, x2, row(ln3_g), row(ln3_b), alpha, tl)
    return x3.reshape(bsz, seq, d)


def kernel(x, mem, w_in, b_in, rnn_conv_w, rnn_conv_b, rnn_w_a, rnn_b_a, rnn_w_x, rnn_b_x, rnn_lambda, w_rnn_out, conf_dw_w, conf_dw_b, conf_ln_g, conf_ln_b, w_conf_out, b_conf_out, w_mix_out, ln1_g, ln1_b, xa_w_q, xa_w_k, xa_w_v, xa_w_o, ln2_g, ln2_b, peer_w_q, peer_sub_keys, peer_u, peer_v, ln3_g, ln3_b):
    depth = w_in.shape[0]
    alpha = (2 * depth) ** 0.25
    params = (w_in, b_in, rnn_conv_w, rnn_conv_b, rnn_w_a, rnn_b_a, rnn_w_x, rnn_b_x, rnn_lambda, w_rnn_out,
              conf_dw_w, conf_dw_b, conf_ln_g, conf_ln_b, w_conf_out, b_conf_out, w_mix_out, ln1_g, ln1_b,
              xa_w_q, xa_w_k, xa_w_v, xa_w_o, ln2_g, ln2_b, peer_w_q, peer_sub_keys, peer_u, peer_v,
              ln3_g, ln3_b)
    for l in range(depth):
        x = _layer(x, mem, *[p[l] for p in params], alpha)
    return x
```

```python
import functools
import math

import jax
import jax.numpy as jnp
from jax import lax
from jax.experimental import pallas as pl
from jax.experimental.pallas import tpu as pltpu

F32 = jnp.float32
BF16 = jnp.bfloat16

LRU_C = 8.0
LN_EPS = 1e-5
XA_HEADS = 4
PEER_TOPK = 16
VMEM_LIMIT_BYTES = 56 * 1024 * 1024
LANES = 128
SUBLANES = 8


def _cparams(*sem):
    return pltpu.CompilerParams(dimension_semantics=sem, vmem_limit_bytes=VMEM_LIMIT_BYTES)


def _layer_norm_rows(z, g, b):
    mu = jnp.mean(z, axis=-1, keepdims=True)
    zc = z - mu
    var = jnp.mean(zc * zc, axis=-1, keepdims=True)
    return zc * lax.rsqrt(var + LN_EPS) * g + b


def _dot(a, b):
    return jnp.dot(a, b, preferred_element_type=F32)


def _cast_weights_once(step, pairs):
    @pl.when(step == 0)
    def _():
        for w_ref, wbf_ref in pairs:
            wbf_ref[...] = w_ref[...].astype(BF16)


def _proj_kernel(a_ref, w_ref, b_ref, o_ref, wbf_ref, *, act):
    _cast_weights_once(pl.program_id(1), [(w_ref, wbf_ref)])
    y = _dot(a_ref[...], wbf_ref[...]) + b_ref[...]
    if act == "gelu":
        y = jax.nn.gelu(y)
    elif act == "sigmoid":
        y = jax.nn.sigmoid(y)
    o_ref[...] = y.astype(o_ref.dtype)


def _proj(a, w, b, col0, ncols, act, out_dtype, tm, tn):
    m, k = a.shape
    c0 = col0 // tn
    return pl.pallas_call(
        functools.partial(_proj_kernel, act=act),
        grid=(ncols // tn, m // tm),
        in_specs=[
            pl.BlockSpec((tm, k), lambda j, i: (i, 0)),
            pl.BlockSpec((k, tn), lambda j, i: (0, j + c0)),
            pl.BlockSpec((1, tn), lambda j, i: (0, j + c0)),
        ],
        out_specs=pl.BlockSpec((tm, tn), lambda j, i: (i, j)),
        out_shape=jax.ShapeDtypeStruct((m, ncols), out_dtype),
        scratch_shapes=[pltpu.VMEM((k, tn), BF16)],
        compiler_params=_cparams("parallel", "arbitrary"),
        name="proj_" + act,
    )(a, w, b)


def _glu_kernel(a_ref, w1_ref, w2_ref, b1_ref, b2_ref, o_ref, w1bf_ref, w2bf_ref):
    _cast_weights_once(pl.program_id(1), [(w1_ref, w1bf_ref), (w2_ref, w2bf_ref)])
    a = a_ref[...]
    y1 = _dot(a, w1bf_ref[...]) + b1_ref[...]
    y2 = _dot(a, w2bf_ref[...]) + b2_ref[...]
    o_ref[...] = (y1 * jax.nn.sigmoid(y2)).astype(o_ref.dtype)


def _glu(a, w, b, col1, col2, ncols, tm, tn):
    m, k = a.shape
    c1, c2 = col1 // tn, col2 // tn
    return pl.pallas_call(
        _glu_kernel,
        grid=(ncols // tn, m // tm),
        in_specs=[
            pl.BlockSpec((tm, k), lambda j, i: (i, 0)),
            pl.BlockSpec((k, tn), lambda j, i: (0, j + c1)),
            pl.BlockSpec((k, tn), lambda j, i: (0, j + c2)),
            pl.BlockSpec((1, tn), lambda j, i: (0, j + c1)),
            pl.BlockSpec((1, tn), lambda j, i: (0, j + c2)),
        ],
        out_specs=pl.BlockSpec((tm, tn), lambda j, i: (i, j)),
        out_shape=jax.ShapeDtypeStruct((m, ncols), F32),
        scratch_shapes=[pltpu.VMEM((k, tn), BF16), pltpu.VMEM((k, tn), BF16)],
        compiler_params=_cparams("parallel", "arbitrary"),
        name="proj_glu",
    )(a, w, w, b, b)


def _rnn_kernel(xr_ref, cw_ref, cb_ref, wa_ref, wx_ref, ba_ref, bx_ref, lam_ref, gg_ref, o_ref,
                ext_ref, a_ref, u_ref, h_ref):
    ts, ct = xr_ref.shape
    kw = cw_ref.shape[0]
    s_idx = pl.program_id(2)

    @pl.when(s_idx == 0)
    def _():
        ext_ref[0:SUBLANES, :] = jnp.zeros((SUBLANES, ct), F32)
        h_ref[...] = jnp.zeros_like(h_ref)

    ext_ref[SUBLANES:SUBLANES + ts, :] = xr_ref[...]
    y = jnp.zeros((ts, ct), F32) + cb_ref[...]
    ext = ext_ref[...]
    for k in range(kw):
        off = SUBLANES - (kw - 1) + k
        z = ext if off % SUBLANES == 0 else pltpu.roll(ext, ts + SUBLANES - off % SUBLANES, 0)
        base = off - off % SUBLANES
        y = y + cw_ref[k:k + 1, :] * z[base:base + ts, :]
    ext_ref[0:SUBLANES, :] = ext_ref[ts:ts + SUBLANES, :]

    nblk = ct // LANES
    r_parts, i_parts = [], []
    for blk in range(nblk):
        yb = y[:, blk * LANES:(blk + 1) * LANES].astype(BF16)
        r_parts.append(_dot(yb, wa_ref[blk]))
        i_parts.append(_dot(yb, wx_ref[blk]))
    r = jax.nn.sigmoid(jnp.concatenate(r_parts, axis=1) + ba_ref[...])
    ig = jax.nn.sigmoid(jnp.concatenate(i_parts, axis=1) + bx_ref[...])
    lam = lam_ref[...]
    sp = jnp.maximum(-lam, 0.0) + jnp.log1p(jnp.exp(-jnp.abs(lam)))
    log_a = (-LRU_C * r) * sp
    a = jnp.exp(log_a)
    u = jnp.sqrt(1.0 - jnp.exp(2.0 * log_a)) * (ig * y)
    a_ref[...] = a
    u_ref[...] = u

    row = lax.broadcasted_iota(jnp.int32, (SUBLANES, ct), 0)

    def body(g, h):
        r0 = pl.multiple_of(g * SUBLANES, SUBLANES)
        av = a_ref[pl.ds(r0, SUBLANES), :]
        uv = u_ref[pl.ds(r0, SUBLANES), :]
        for d in (1, 2, 4):
            a_sh = jnp.where(row >= d, pltpu.roll(av, d, 0), 1.0)
            u_sh = jnp.where(row >= d, pltpu.roll(uv, d, 0), 0.0)
            uv = av * u_sh + uv
            av = av * a_sh
        hh = uv + av * h
        u_ref[pl.ds(r0, SUBLANES), :] = hh
        return hh[SUBLANES - 1:SUBLANES, :]

    h_last = lax.fori_loop(0, ts // SUBLANES, body, h_ref[...], unroll=4)
    h_ref[...] = h_last
    o_ref[...] = (u_ref[...] * gg_ref[...].astype(F32)).astype(o_ref.dtype)


def _rnn_branch(xr, cw, cb, wa, wx, ba, bx, lam, gg, ts, ct):
    bsz, seq, c = xr.shape
    kw = cw.shape[0]
    assert kw - 1 <= SUBLANES
    nb = ct // LANES
    vec = pl.BlockSpec((1, ct), lambda b, j, s: (0, j))
    return pl.pallas_call(
        _rnn_kernel,
        grid=(bsz, c // ct, seq // ts),
        in_specs=[
            pl.BlockSpec((None, ts, ct), lambda b, j, s: (b, s, j)),
            pl.BlockSpec((kw, ct), lambda b, j, s: (0, j)),
            vec,
            pl.BlockSpec((nb, LANES, LANES), lambda b, j, s: (j, 0, 0)),
            pl.BlockSpec((nb, LANES, LANES), lambda b, j, s: (j, 0, 0)),
            vec, vec, vec,
            pl.BlockSpec((None, ts, ct), lambda b, j, s: (b, s, j)),
        ],
        out_specs=pl.BlockSpec((None, ts, ct), lambda b, j, s: (b, s, j)),
        out_shape=jax.ShapeDtypeStruct((bsz, seq, c), BF16),
        scratch_shapes=[
            pltpu.VMEM((ts + SUBLANES, ct), F32),
            pltpu.VMEM((ts, ct), F32),
            pltpu.VMEM((ts, ct), F32),
            pltpu.VMEM((1, ct), F32),
        ],
        compiler_params=_cparams("parallel", "parallel", "arbitrary"),
        name="rnn_branch",
    )(xr, cw, cb, wa, wx, ba, bx, lam, gg)


CONV_HALO = 32
CONV_LANE_CHUNK = 128
CONV_ROW_BLOCK = 64


def _conf_kernel(c_ref, dw_ref, db_ref, g_ref, b_ref, o_ref, ext_ref, y_ref):
    ts, ch = c_ref.shape
    kw = dw_ref.shape[0]
    s_idx = pl.program_id(1)

    @pl.when(s_idx == 0)
    def _():
        ext_ref[0:CONV_HALO, :] = jnp.zeros((CONV_HALO, ch), F32)

    ext_ref[CONV_HALO:CONV_HALO + ts, :] = c_ref[...]
    off0 = CONV_HALO - (kw - 1)
    rb_rows = min(CONV_ROW_BLOCK, ts)
    win_rows = CONV_HALO + rb_rows
    lc = min(CONV_LANE_CHUNK, ch)
    n_lane = ch // lc

    def body(it, carry):
        rb = it // n_lane
        lb = it % n_lane
        r0 = pl.multiple_of(rb * rb_rows, rb_rows)
        l0 = pl.multiple_of(lb * lc, lc)
        win = ext_ref[pl.ds(r0, win_rows), pl.ds(l0, lc)]
        acc = jnp.zeros((rb_rows, lc), F32) + db_ref[:, pl.ds(l0, lc)]
        for r in range(SUBLANES):
            z = win if r == 0 else pltpu.roll(win, win_rows - r, 0)
            for q in range(win_rows // SUBLANES):
                k = q * SUBLANES + r - off0
                if 0 <= k < kw:
                    acc = acc + dw_ref[k:k + 1, pl.ds(l0, lc)] * z[q * SUBLANES:q * SUBLANES + rb_rows, :]
        y_ref[pl.ds(r0, rb_rows), pl.ds(l0, lc)] = acc
        return carry

    lax.fori_loop(0, (ts // rb_rows) * n_lane, body, 0)
    ext_ref[0:CONV_HALO, :] = ext_ref[ts:ts + CONV_HALO, :]
    yn = _layer_norm_rows(y_ref[...], g_ref[...], b_ref[...])
    o_ref[...] = (yn * jax.nn.sigmoid(yn)).astype(o_ref.dtype)


def _conf_branch(c, dw, db, g, b, ts):
    bsz, seq, ch = c.shape
    kw = dw.shape[0]
    assert kw - 1 <= CONV_HALO
    vec = pl.BlockSpec((1, ch), lambda bi, s: (0, 0))
    return pl.pallas_call(
        _conf_kernel,
        grid=(bsz, seq // ts),
        in_specs=[
            pl.BlockSpec((None, ts, ch), lambda bi, s: (bi, s, 0)),
            pl.BlockSpec((kw, ch), lambda bi, s: (0, 0)),
            vec, vec, vec,
        ],
        out_specs=pl.BlockSpec((None, ts, ch), lambda bi, s: (bi, s, 0)),
        out_shape=jax.ShapeDtypeStruct((bsz, seq, ch), BF16),
        scratch_shapes=[
            pltpu.VMEM((ts + CONV_HALO, ch), F32),
            pltpu.VMEM((ts, ch), F32),
        ],
        compiler_params=_cparams("parallel", "arbitrary"),
        name="conf_branch",
    )(c, dw, db, g, b)


def _merge_kernel(hr_ref, c_ref, wr_ref, wc_ref, bc_ref, gr_ref, gc_ref, o_ref, wrbf_ref, wcbf_ref):
    _cast_weights_once(pl.program_id(1), [(wr_ref, wrbf_ref), (wc_ref, wcbf_ref)])
    yr = _dot(hr_ref[...], wrbf_ref[...])
    yc = _dot(c_ref[...], wcbf_ref[...]) + bc_ref[...]
    o_ref[...] = (gr_ref[...].astype(F32) * yr + gc_ref[...].astype(F32) * yc).astype(o_ref.dtype)


def _merge(hr, cact, wr, wc, bc, gates, d_model, tm, tn):
    m, k = hr.shape
    goff = d_model // tn
    return pl.pallas_call(
        _merge_kernel,
        grid=(d_model // tn, m // tm),
        in_specs=[
            pl.BlockSpec((tm, k), lambda j, i: (i, 0)),
            pl.BlockSpec((tm, k), lambda j, i: (i, 0)),
            pl.BlockSpec((k, tn), lambda j, i: (0, j)),
            pl.BlockSpec((k, tn), lambda j, i: (0, j)),
            pl.BlockSpec((1, tn), lambda j, i: (0, j)),
            pl.BlockSpec((tm, tn), lambda j, i: (i, j)),
            pl.BlockSpec((tm, tn), lambda j, i: (i, j + goff)),
        ],
        out_specs=pl.BlockSpec((tm, tn), lambda j, i: (i, j)),
        out_shape=jax.ShapeDtypeStruct((m, d_model), BF16),
        scratch_shapes=[pltpu.VMEM((k, tn), BF16), pltpu.VMEM((k, tn), BF16)],
        compiler_params=_cparams("parallel", "arbitrary"),
        name="merge",
    )(hr, cact, wr, wc, bc, gates, gates)


def _mm_ln_kernel(a_ref, w_ref, res_ref, g_ref, b_ref, o_ref, obf_ref, wbf_ref, *, alpha):
    _cast_weights_once(pl.program_id(0), [(w_ref, wbf_ref)])
    y = _dot(a_ref[...], wbf_ref[...])
    z = _layer_norm_rows(alpha * res_ref[...] + y, g_ref[...], b_ref[...])
    o_ref[...] = z
    obf_ref[...] = z.astype(BF16)


def _mm_ln(a, w, res, g, b, alpha, tm):
    m, k = a.shape
    n = w.shape[1]
    vec = pl.BlockSpec((1, n), lambda i: (0, 0))
    return pl.pallas_call(
        functools.partial(_mm_ln_kernel, alpha=alpha),
        grid=(m // tm,),
        in_specs=[
            pl.BlockSpec((tm, k), lambda i: (i, 0)),
            pl.BlockSpec((k, n), lambda i: (0, 0), pipeline_mode=pl.Buffered(1)),
            pl.BlockSpec((tm, n), lambda i: (i, 0)),
            vec, vec,
        ],
        out_specs=[pl.BlockSpec((tm, n), lambda i: (i, 0)), pl.BlockSpec((tm, n), lambda i: (i, 0))],
        out_shape=[jax.ShapeDtypeStruct((m, n), F32), jax.ShapeDtypeStruct((m, n), BF16)],
        scratch_shapes=[pltpu.VMEM((k, n), BF16)],
        compiler_params=_cparams("arbitrary"),
        name="mm_ln",
    )(a, w, res, g, b)


def _add_ln_kernel(y_ref, res_ref, g_ref, b_ref, o_ref, *, alpha):
    o_ref[...] = _layer_norm_rows(alpha * res_ref[...] + y_ref[...], g_ref[...], b_ref[...])


def _add_ln(y, res, g, b, alpha, tm):
    m, n = y.shape
    vec = pl.BlockSpec((1, n), lambda i: (0, 0))
    return pl.pallas_call(
        functools.partial(_add_ln_kernel, alpha=alpha),
        grid=(m // tm,),
        in_specs=[pl.BlockSpec((tm, n), lambda i: (i, 0)), pl.BlockSpec((tm, n), lambda i: (i, 0)), vec, vec],
        out_specs=pl.BlockSpec((tm, n), lambda i: (i, 0)),
        out_shape=jax.ShapeDtypeStruct((m, n), F32),
        compiler_params=_cparams("parallel"),
        name="add_ln",
    )(y, res, g, b)


def _attn_kernel(q_ref, k_ref, v_ref, o_ref, *, heads):
    tq, d = q_ref.shape
    hd = d // heads
    scale = hd ** -0.5
    outs = []
    for h in range(heads):
        sl = slice(h * hd, (h + 1) * hd)
        s = lax.dot_general(q_ref[:, sl], k_ref[:, sl], (((1,), (1,)), ((), ())),
                            preferred_element_type=F32) * scale
        m = jnp.max(s, axis=-1, keepdims=True)
        p = jnp.exp(s - m)
        p = p / jnp.sum(p, axis=-1, keepdims=True)
        outs.append(_dot(p.astype(BF16), v_ref[:, sl]))
    o_ref[...] = jnp.concatenate(outs, axis=1).astype(o_ref.dtype)


def _attention(q, k, v, bsz, seq, mem_len, d_model, tq):
    nq = seq // tq
    return pl.pallas_call(
        functools.partial(_attn_kernel, heads=XA_HEADS),
        grid=(bsz, nq),
        in_specs=[
            pl.BlockSpec((tq, d_model), lambda b, i: (b * nq + i, 0)),
            pl.BlockSpec((mem_len, d_model), lambda b, i: (b, 0)),
            pl.BlockSpec((mem_len, d_model), lambda b, i: (b, 0)),
        ],
        out_specs=pl.BlockSpec((tq, d_model), lambda b, i: (b * nq + i, 0)),
        out_shape=jax.ShapeDtypeStruct((bsz * seq, d_model), BF16),
        compiler_params=_cparams("parallel", "parallel"),
        name="xattn",
    )(q, k, v)


def _extract_topk(s, k, pos, exact):
    rank = jnp.full(s.shape, float(k), F32)
    vals = []
    for r in range(k):
        m = jnp.max(s, axis=0, keepdims=True)
        hit = s == m
        if exact:
            first = jnp.min(jnp.where(hit, pos, jnp.inf), axis=0, keepdims=True)
            hit = pos == first
        rank = jnp.where(hit, float(r), rank)
        s = jnp.where(hit, -jnp.inf, s)
        vals.append(m)
    n_ranked = jnp.sum(jnp.where(rank < float(k), 1.0, 0.0), axis=0, keepdims=True)
    return vals, rank, n_ranked == float(k)


def _candidate_blocks(k):
    blocks = [("row", 0, k), ("row", 1, k // 2)]
    b = 0
    while 3 * (b + 1) <= k:
        hi = k // (b + 1)
        blocks.append(("col", b, -(-hi // SUBLANES) * SUBLANES, 2, hi))
        b += 1
    return blocks


def _choose_pairs(v1, v2, rank1, k, exact):
    tt = rank1.shape[1]
    v1_all = jnp.concatenate(v1, axis=0)
    v2_all = jnp.concatenate(v2, axis=0)
    e1_all = jnp.exp(v1_all - v1[0])
    e2_all = jnp.exp(v2_all - v2[0])
    cands, prods, poss = [], [], []
    for blk in _candidate_blocks(k):
        if blk[0] == "row":
            _, a, n = blk
            cands.append(v1[a] + v2_all[0:n])
            prods.append(e1_all[a:a + 1] * e2_all[0:n])
            poss.append((a * k + lax.broadcasted_iota(jnp.int32, (n, tt), 0)).astype(F32))
        else:
            _, b, n, lo, hi = blk
            a_col = lax.broadcasted_iota(jnp.int32, (n, tt), 0)
            valid = jnp.logical_and(a_col >= lo, a_col < hi)
            cands.append(jnp.where(valid, v1_all[0:n] + v2[b], -jnp.inf))
            prods.append(e1_all[0:n] * e2_all[b:b + 1])
            poss.append(jnp.where(valid, (a_col * k + b).astype(F32), jnp.inf))
    _, rankc, okc = _extract_topk(jnp.concatenate(cands, axis=0), k, jnp.concatenate(poss, axis=0), exact)
    sel = jnp.where(rankc < float(k), 1.0, 0.0)
    z = jnp.sum(sel * jnp.concatenate(prods, axis=0), axis=0, keepdims=True)
    cnt = jnp.zeros((k, tt), F32)
    a_col = lax.broadcasted_iota(jnp.int32, (k, tt), 0)
    row0 = 0
    for blk in _candidate_blocks(k):
        n = blk[2]
        part = sel[row0:row0 + n]
        row0 += n
        if blk[0] == "row":
            cnt = cnt + jnp.where(a_col == blk[1], jnp.sum(part, axis=0, keepdims=True), 0.0)
        elif n == k:
            cnt = cnt + part
        else:
            cnt = cnt + jnp.concatenate([part, jnp.zeros((k - n, part.shape[1]), F32)], axis=0)
    c1 = jnp.zeros_like(rank1)
    for a in range(k):
        c1 = jnp.where(rank1 == float(a), cnt[a:a + 1], c1)
    return c1, z, okc


def _route_head(s1, s2, k, exact):
    key_pos = lax.broadcasted_iota(jnp.int32, s1.shape, 0).astype(F32)
    v1, rank1, ok1 = _extract_topk(s1, k, key_pos, exact)
    v2, rank2, ok2 = _extract_topk(s2, k, key_pos, exact)
    c1, z, _ = _choose_pairs(v1, v2, rank1, k, True)
    ok = jnp.logical_and(ok1, ok2)
    return c1, jnp.exp(s1 - v1[0]), rank2, jnp.exp(s2 - v2[0]) / z, ok


def _peer_route_kernel(x_ref, wq_ref, keys_ref, c1_ref, e1_ref, r2_ref, e2_ref, wqbf_ref, *, heads, topk):
    n_keys, half = keys_ref.shape[1], keys_ref.shape[2]
    _cast_weights_once(pl.program_id(0), [(wq_ref, wqbf_ref)])
    q = _dot(x_ref[...], wqbf_ref[...])
    nt = (((1,), (1,)), ((), ()))

    def route_all(exact):
        ok = None
        for h in range(heads):
            base = h * 2 * half
            s1 = lax.dot_general(keys_ref[0], q[:, base:base + half], nt, preferred_element_type=F32)
            s2 = lax.dot_general(keys_ref[1], q[:, base + half:base + 2 * half], nt, preferred_element_type=F32)
            c1, e1, rank2, e2, ok_h = _route_head(s1, s2, topk, exact)
            c1_ref[h] = _dup_bf16_words(c1)
            e1_ref[h] = _dup_bf16_words(e1)
            r2_ref[h] = rank2.astype(BF16)
            e2_ref[h] = e2.astype(BF16)
            ok = ok_h if ok is None else jnp.logical_and(ok, ok_h)
        return ok

    ok = route_all(exact=False)
    n_bad = jnp.sum(jnp.where(ok, 0.0, 1.0))

    @pl.when(n_bad > 0.0)
    def _():
        route_all(exact=True)


def _dup_bf16_words(v):
    b = lax.bitcast_convert_type(v.astype(BF16).astype(F32), jnp.uint32)
    return b | (b >> 16)


def _peer_route(x_bf, wq, keys, heads, tt):
    t, d = x_bf.shape
    n_keys = keys.shape[1]
    out_w = jax.ShapeDtypeStruct((heads, n_keys, t), jnp.uint32)
    out_b = jax.ShapeDtypeStruct((heads, n_keys, t), BF16)
    ospec = pl.BlockSpec((heads, n_keys, tt), lambda i: (0, 0, i))
    return pl.pallas_call(
        functools.partial(_peer_route_kernel, heads=heads, topk=PEER_TOPK),
        grid=(t // tt,),
        in_specs=[
            pl.BlockSpec((tt, d), lambda i: (i, 0)),
            pl.BlockSpec(wq.shape, lambda i: (0, 0)),
            pl.BlockSpec(keys.shape, lambda i: (0, 0, 0)),
        ],
        out_specs=[ospec, ospec, ospec, ospec],
        out_shape=[out_w, out_w, out_b, out_b],
        scratch_shapes=[pltpu.VMEM(wq.shape, BF16)],
        compiler_params=_cparams("arbitrary"),
        name="peer_route",
    )(x_bf, wq, keys)


def _peer_dense_kernel(xn_ref, un_ref, vt_ref, c1_ref, e1_ref, r2_ref, e2_ref, o_ref, acc_ref, ht_ref, act_ref, *,
                       heads, n_e):
    s_idx = pl.program_id(0)
    e_gate = jnp.maximum(s_idx - 1, 0) % n_e
    e_down = jnp.maximum(s_idx - 2, 0) % n_e
    ec = un_ref.shape[0]
    tt = xn_ref.shape[0]
    n_keys = r2_ref.shape[1]
    rows_per_step = ec // n_keys

    @pl.when(s_idx == 0)
    def _():
        ht_ref[...] = jnp.zeros_like(ht_ref)
        act_ref[...] = jnp.zeros_like(act_ref)

    @pl.when(e_down == 0)
    def _():
        acc_ref[...] = jnp.zeros_like(acc_ref)

    acc_ref[...] += _dot(vt_ref[...], act_ref[...])
    ht = ht_ref[...]
    ht_ref[...] = lax.dot_general(un_ref[...], xn_ref[...], (((1,), (1,)), ((), ())), preferred_element_type=F32)

    def row_bf16(ref, h, i):
        words = jnp.broadcast_to(ref[h, pl.ds(i, 1), :], (n_keys // 2, tt))
        return pltpu.bitcast(words, BF16)

    for ii in range(rows_per_step):
        i = e_gate * rows_per_step + ii
        g = None
        for h in range(heads):
            sel = jnp.where(r2_ref[h] < row_bf16(c1_ref, h, i), e2_ref[h], jnp.zeros((), BF16))
            term = sel * row_bf16(e1_ref, h, i)
            g = term if g is None else g + term
        rows = slice(ii * n_keys, (ii + 1) * n_keys)
        act_ref[rows, :] = jax.nn.gelu(ht[rows, :]).astype(BF16) * g

    @pl.when(jnp.logical_and(e_down == n_e - 1, s_idx > 1))
    def _():
        o_ref[...] = acc_ref[...].T


def _peer_dense(x_bf, u_bf, vt_bf, c1, e1, r2, e2, tt, ec):
    t, d = x_bf.shape
    n_exp = u_bf.shape[0]
    heads, n_keys, _ = c1.shape
    n_t, n_e = t // tt, n_exp // ec
    n_chunks = n_t * n_e
    up = lambda s: jnp.minimum(s, n_chunks - 1)
    gate = lambda s: jnp.clip(s - 1, 0, n_chunks - 1)
    down = lambda s: jnp.maximum(s - 2, 0)
    rspec = pl.BlockSpec((heads, n_keys, tt), lambda s: (0, 0, gate(s) // n_e))
    return pl.pallas_call(
        functools.partial(_peer_dense_kernel, heads=heads, n_e=n_e),
        grid=(n_chunks + 2,),
        in_specs=[
            pl.BlockSpec((tt, d), lambda s: (up(s) // n_e, 0)),
            pl.BlockSpec((ec, d), lambda s: (up(s) % n_e, 0)),
            pl.BlockSpec((None, d, ec), lambda s: (down(s) % n_e, 0, 0)),
            rspec, rspec, rspec, rspec,
        ],
        out_specs=pl.BlockSpec((tt, d), lambda s: (down(s) // n_e, 0)),
        out_shape=jax.ShapeDtypeStruct((t, d), F32),
        scratch_shapes=[pltpu.VMEM((d, tt), F32), pltpu.VMEM((ec, tt), F32), pltpu.VMEM((ec, tt), BF16)],
        compiler_params=_cparams("arbitrary"),
        name="peer_dense",
    )(x_bf, u_bf, vt_bf, c1, e1, r2, e2)


def _transpose_cast_kernel(v_ref, o_ref):
    o_ref[...] = v_ref[...].T.astype(o_ref.dtype)


def _transpose_cast(v, te):
    n_exp, d = v.shape
    return pl.pallas_call(
        _transpose_cast_kernel,
        grid=(n_exp // te,),
        in_specs=[pl.BlockSpec((te, d), lambda e: (e, 0))],
        out_specs=pl.BlockSpec((None, d, te), lambda e: (e, 0, 0)),
        out_shape=jax.ShapeDtypeStruct((n_exp // te, d, te), BF16),
        compiler_params=_cparams("parallel"),
        name="transpose_cast",
    )(v)


def _tile(n, pref):
    t = min(n, pref)
    assert n % t == 0
    return t


def _layer(x, mem, w_in, b_in, rnn_conv_w, rnn_conv_b, rnn_w_a, rnn_b_a, rnn_w_x, rnn_b_x, rnn_lambda,
           w_rnn_out, conf_dw_w, conf_dw_b, conf_ln_g, conf_ln_b, w_conf_out, b_conf_out, w_mix_out,
           ln1_g, ln1_b, xa_w_q, xa_w_k, xa_w_v, xa_w_o, ln2_g, ln2_b,
           peer_w_q, peer_sub_keys, peer_u, peer_v, ln3_g, ln3_b, alpha):
    bsz, seq, d = x.shape
    t = bsz * seq
    mem_len = mem.shape[1]
    d_rnn = rnn_conv_w.shape[1]
    d_conv = conf_dw_w.shape[1]
    row = lambda v: v.reshape(1, -1)

    tm = _tile(t, 1024)
    tn = _tile(d, 1024)
    x2d = x.reshape(t, d)
    x_bf = x2d.astype(BF16)
    b_in2 = row(b_in)
    tn2 = _tile(d, 512)

    xr = _proj(x_bf, w_in, b_in2, 0, d_rnn, "none", F32, tm, tn)
    gg = _proj(x_bf, w_in, b_in2, d_rnn, d_rnn, "gelu", BF16, tm, tn)
    cglu = _glu(x_bf, w_in, b_in2, 2 * d_rnn, 2 * d_rnn + d_conv, d_conv, tm, tn2)
    mgates = _proj(x_bf, w_in, b_in2, 2 * d_rnn + 2 * d_conv, 2 * d, "sigmoid", BF16, tm, tn)

    hr = _rnn_branch(xr.reshape(bsz, seq, d_rnn), rnn_conv_w, row(rnn_conv_b),
                     rnn_w_a.astype(BF16), rnn_w_x.astype(BF16), row(rnn_b_a), row(rnn_b_x),
                     row(rnn_lambda), gg.reshape(bsz, seq, d_rnn),
                     _tile(seq, 512), _tile(d_rnn, 512))
    cact = _conf_branch(cglu.reshape(bsz, seq, d_conv), conf_dw_w, row(conf_dw_b),
                        row(conf_ln_g), row(conf_ln_b), _tile(seq, 256))

    merged = _merge(hr.reshape(t, d_rnn), cact.reshape(t, d_conv), w_rnn_out,
                    w_conf_out, row(b_conf_out), mgates, d, tm, tn2)
    tl = _tile(t, 512)
    x1, x1_bf = _mm_ln(merged, w_mix_out, x2d, row(ln1_g), row(ln1_b), alpha, tl)

    zero_d = jnp.zeros((1, d), F32)
    q = _proj(x1_bf, xa_w_q, zero_d, 0, d, "none", BF16, tm, tn)
    mem_bf = mem.reshape(bsz * mem_len, d).astype(BF16)
    tmem = _tile(bsz * mem_len, 1024)
    k = _proj(mem_bf, xa_w_k, zero_d, 0, d, "none", BF16, tmem, tn)
    v = _proj(mem_bf, xa_w_v, zero_d, 0, d, "none", BF16, tmem, tn)
    o = _attention(q, k, v, bsz, seq, mem_len, d, _tile(seq, 512))
    x2, x2_bf = _mm_ln(o, xa_w_o, x1, row(ln2_g), row(ln2_b), alpha, tl)

    heads = peer_w_q.shape[1] // (2 * peer_sub_keys.shape[2])
    c1, e1, r2, e2 = _peer_route(x2_bf, peer_w_q, peer_sub_keys, heads, _tile(t, 256))
    n_exp = peer_u.shape[0]
    ec = _tile(n_exp, 1024)
    vt_bf = _transpose_cast(peer_v, ec)
    ff = _peer_dense(x2_bf, peer_u.astype(BF16), vt_bf, c1, e1, r2, e2, _tile(t, 512), ec)
    x3 = _add_ln(ff, x2, row(ln3_g), row(ln3_b), alpha, tl)
    return x3.reshape(bsz, seq, d)


def kernel(x, mem, w_in, b_in, rnn_conv_w, rnn_conv_b, rnn_w_a, rnn_b_a, rnn_w_x, rnn_b_x, rnn_lambda, w_rnn_out, conf_dw_w, conf_dw_b, conf_ln_g, conf_ln_b, w_conf_out, b_conf_out, w_mix_out, ln1_g, ln1_b, xa_w_q, xa_w_k, xa_w_v, xa_w_o, ln2_g, ln2_b, peer_w_q, peer_sub_keys, peer_u, peer_v, ln3_g, ln3_b):
    depth = w_in.shape[0]
    alpha = (2 * depth) ** 0.25
    params = (w_in, b_in, rnn_conv_w, rnn_conv_b, rnn_w_a, rnn_b_a, rnn_w_x, rnn_b_x, rnn_lambda, w_rnn_out,
              conf_dw_w, conf_dw_b, conf_ln_g, conf_ln_b, w_conf_out, b_conf_out, w_mix_out, ln1_g, ln1_b,
              xa_w_q, xa_w_k, xa_w_v, xa_w_o, ln2_g, ln2_b, peer_w_q, peer_sub_keys, peer_u, peer_v,
              ln3_g, ln3_b)
    for l in range(depth):
        x = _layer(x, mem, *[p[l] for p in params], alpha)
    return x
```

```python
import functools
import math

import jax
import jax.numpy as jnp
from jax import lax
from jax.experimental import pallas as pl
from jax.experimental.pallas import tpu as pltpu

F32 = jnp.float32
BF16 = jnp.bfloat16

LRU_C = 8.0
LN_EPS = 1e-5
XA_HEADS = 4
PEER_TOPK = 16
VMEM_LIMIT_BYTES = 56 * 1024 * 1024
LANES = 128
SUBLANES = 8


def _cparams(*sem):
    return pltpu.CompilerParams(dimension_semantics=sem, vmem_limit_bytes=VMEM_LIMIT_BYTES)


def _layer_norm_rows(z, g, b):
    mu = jnp.mean(z, axis=-1, keepdims=True)
    zc = z - mu
    var = jnp.mean(zc * zc, axis=-1, keepdims=True)
    return zc * lax.rsqrt(var + LN_EPS) * g + b


def _dot(a, b):
    return jnp.dot(a, b, preferred_element_type=F32)


def _cast_weights_once(step, pairs):
    @pl.when(step == 0)
    def _():
        for w_ref, wbf_ref in pairs:
            wbf_ref[...] = w_ref[...].astype(BF16)


def _proj_kernel(a_ref, w_ref, b_ref, o_ref, wbf_ref, *, act):
    _cast_weights_once(pl.program_id(1), [(w_ref, wbf_ref)])
    y = _dot(a_ref[...], wbf_ref[...]) + b_ref[...]
    if act == "gelu":
        y = jax.nn.gelu(y)
    elif act == "sigmoid":
        y = jax.nn.sigmoid(y)
    o_ref[...] = y.astype(o_ref.dtype)


def _proj(a, w, b, col0, ncols, act, out_dtype, tm, tn):
    m, k = a.shape
    c0 = col0 // tn
    return pl.pallas_call(
        functools.partial(_proj_kernel, act=act),
        grid=(ncols // tn, m // tm),
        in_specs=[
            pl.BlockSpec((tm, k), lambda j, i: (i, 0)),
            pl.BlockSpec((k, tn), lambda j, i: (0, j + c0)),
            pl.BlockSpec((1, tn), lambda j, i: (0, j + c0)),
        ],
        out_specs=pl.BlockSpec((tm, tn), lambda j, i: (i, j)),
        out_shape=jax.ShapeDtypeStruct((m, ncols), out_dtype),
        scratch_shapes=[pltpu.VMEM((k, tn), BF16)],
        compiler_params=_cparams("parallel", "arbitrary"),
        name="proj_" + act,
    )(a, w, b)


def _glu_kernel(a_ref, w1_ref, w2_ref, b1_ref, b2_ref, o_ref, w1bf_ref, w2bf_ref):
    _cast_weights_once(pl.program_id(1), [(w1_ref, w1bf_ref), (w2_ref, w2bf_ref)])
    a = a_ref[...]
    y1 = _dot(a, w1bf_ref[...]) + b1_ref[...]
    y2 = _dot(a, w2bf_ref[...]) + b2_ref[...]
    o_ref[...] = (y1 * jax.nn.sigmoid(y2)).astype(o_ref.dtype)


def _glu(a, w, b, col1, col2, ncols, tm, tn):
    m, k = a.shape
    c1, c2 = col1 // tn, col2 // tn
    return pl.pallas_call(
        _glu_kernel,
        grid=(ncols // tn, m // tm),
        in_specs=[
            pl.BlockSpec((tm, k), lambda j, i: (i, 0)),
            pl.BlockSpec((k, tn), lambda j, i: (0, j + c1)),
            pl.BlockSpec((k, tn), lambda j, i: (0, j + c2)),
            pl.BlockSpec((1, tn), lambda j, i: (0, j + c1)),
            pl.BlockSpec((1, tn), lambda j, i: (0, j + c2)),
        ],
        out_specs=pl.BlockSpec((tm, tn), lambda j, i: (i, j)),
        out_shape=jax.ShapeDtypeStruct((m, ncols), F32),
        scratch_shapes=[pltpu.VMEM((k, tn), BF16), pltpu.VMEM((k, tn), BF16)],
        compiler_params=_cparams("parallel", "arbitrary"),
        name="proj_glu",
    )(a, w, w, b, b)


def _rnn_kernel(xr_ref, cw_ref, cb_ref, wa_ref, wx_ref, ba_ref, bx_ref, lam_ref, gg_ref, o_ref,
                ext_ref, a_ref, u_ref, h_ref):
    ts, ct = xr_ref.shape
    kw = cw_ref.shape[0]
    s_idx = pl.program_id(2)

    @pl.when(s_idx == 0)
    def _():
        ext_ref[0:SUBLANES, :] = jnp.zeros((SUBLANES, ct), F32)
        h_ref[...] = jnp.zeros_like(h_ref)

    ext_ref[SUBLANES:SUBLANES + ts, :] = xr_ref[...]
    y = jnp.zeros((ts, ct), F32) + cb_ref[...]
    ext = ext_ref[...]
    for k in range(kw):
        off = SUBLANES - (kw - 1) + k
        z = ext if off % SUBLANES == 0 else pltpu.roll(ext, ts + SUBLANES - off % SUBLANES, 0)
        base = off - off % SUBLANES
        y = y + cw_ref[k:k + 1, :] * z[base:base + ts, :]
    ext_ref[0:SUBLANES, :] = ext_ref[ts:ts + SUBLANES, :]

    nblk = ct // LANES
    r_parts, i_parts = [], []
    for blk in range(nblk):
        yb = y[:, blk * LANES:(blk + 1) * LANES].astype(BF16)
        r_parts.append(_dot(yb, wa_ref[blk]))
        i_parts.append(_dot(yb, wx_ref[blk]))
    r = jax.nn.sigmoid(jnp.concatenate(r_parts, axis=1) + ba_ref[...])
    ig = jax.nn.sigmoid(jnp.concatenate(i_parts, axis=1) + bx_ref[...])
    lam = lam_ref[...]
    sp = jnp.maximum(-lam, 0.0) + jnp.log1p(jnp.exp(-jnp.abs(lam)))
    log_a = (-LRU_C * r) * sp
    a = jnp.exp(log_a)
    u = jnp.sqrt(1.0 - jnp.exp(2.0 * log_a)) * (ig * y)
    a_ref[...] = a
    u_ref[...] = u

    row = lax.broadcasted_iota(jnp.int32, (SUBLANES, ct), 0)

    def body(g, h):
        r0 = pl.multiple_of(g * SUBLANES, SUBLANES)
        av = a_ref[pl.ds(r0, SUBLANES), :]
        uv = u_ref[pl.ds(r0, SUBLANES), :]
        for d in (1, 2, 4):
            a_sh = jnp.where(row >= d, pltpu.roll(av, d, 0), 1.0)
            u_sh = jnp.where(row >= d, pltpu.roll(uv, d, 0), 0.0)
            uv = av * u_sh + uv
            av = av * a_sh
        hh = uv + av * h
        u_ref[pl.ds(r0, SUBLANES), :] = hh
        return hh[SUBLANES - 1:SUBLANES, :]

    h_last = lax.fori_loop(0, ts // SUBLANES, body, h_ref[...], unroll=4)
    h_ref[...] = h_last
    o_ref[...] = (u_ref[...] * gg_ref[...].astype(F32)).astype(o_ref.dtype)


def _rnn_branch(xr, cw, cb, wa, wx, ba, bx, lam, gg, ts, ct):
    bsz, seq, c = xr.shape
    kw = cw.shape[0]
    assert kw - 1 <= SUBLANES
    nb = ct // LANES
    vec = pl.BlockSpec((1, ct), lambda b, j, s: (0, j))
    return pl.pallas_call(
        _rnn_kernel,
        grid=(bsz, c // ct, seq // ts),
        in_specs=[
            pl.BlockSpec((None, ts, ct), lambda b, j, s: (b, s, j)),
            pl.BlockSpec((kw, ct), lambda b, j, s: (0, j)),
            vec,
            pl.BlockSpec((nb, LANES, LANES), lambda b, j, s: (j, 0, 0)),
            pl.BlockSpec((nb, LANES, LANES), lambda b, j, s: (j, 0, 0)),
            vec, vec, vec,
            pl.BlockSpec((None, ts, ct), lambda b, j, s: (b, s, j)),
        ],
        out_specs=pl.BlockSpec((None, ts, ct), lambda b, j, s: (b, s, j)),
        out_shape=jax.ShapeDtypeStruct((bsz, seq, c), BF16),
        scratch_shapes=[
            pltpu.VMEM((ts + SUBLANES, ct), F32),
            pltpu.VMEM((ts, ct), F32),
            pltpu.VMEM((ts, ct), F32),
            pltpu.VMEM((1, ct), F32),
        ],
        compiler_params=_cparams("parallel", "parallel", "arbitrary"),
        name="rnn_branch",
    )(xr, cw, cb, wa, wx, ba, bx, lam, gg)


CONV_HALO = 32
CONV_LANE_CHUNK = 128
CONV_ROW_BLOCK = 128


def _conf_kernel(c_ref, dw_ref, db_ref, g_ref, b_ref, o_ref, ext_ref, y_ref):
    ts, ch = c_ref.shape
    kw = dw_ref.shape[0]
    s_idx = pl.program_id(1)

    @pl.when(s_idx == 0)
    def _():
        ext_ref[0:CONV_HALO, :] = jnp.zeros((CONV_HALO, ch), F32)

    ext_ref[CONV_HALO:CONV_HALO + ts, :] = c_ref[...]
    off0 = CONV_HALO - (kw - 1)
    rb_rows = min(CONV_ROW_BLOCK, ts)
    win_rows = CONV_HALO + rb_rows
    lc = min(CONV_LANE_CHUNK, ch)
    n_lane = ch // lc

    def body(it, carry):
        rb = it // n_lane
        lb = it % n_lane
        r0 = pl.multiple_of(rb * rb_rows, rb_rows)
        l0 = pl.multiple_of(lb * lc, lc)
        win = ext_ref[pl.ds(r0, win_rows), pl.ds(l0, lc)]
        acc = jnp.zeros((rb_rows, lc), F32) + db_ref[:, pl.ds(l0, lc)]
        for r in range(SUBLANES):
            z = win if r == 0 else pltpu.roll(win, win_rows - r, 0)
            for q in range(win_rows // SUBLANES):
                k = q * SUBLANES + r - off0
                if 0 <= k < kw:
                    acc = acc + dw_ref[k:k + 1, pl.ds(l0, lc)] * z[q * SUBLANES:q * SUBLANES + rb_rows, :]
        y_ref[pl.ds(r0, rb_rows), pl.ds(l0, lc)] = acc
        return carry

    lax.fori_loop(0, (ts // rb_rows) * n_lane, body, 0)
    ext_ref[0:CONV_HALO, :] = ext_ref[ts:ts + CONV_HALO, :]
    yn = _layer_norm_rows(y_ref[...], g_ref[...], b_ref[...])
    o_ref[...] = (yn * jax.nn.sigmoid(yn)).astype(o_ref.dtype)


def _conf_branch(c, dw, db, g, b, ts):
    bsz, seq, ch = c.shape
    kw = dw.shape[0]
    assert kw - 1 <= CONV_HALO
    vec = pl.BlockSpec((1, ch), lambda bi, s: (0, 0))
    return pl.pallas_call(
        _conf_kernel,
        grid=(bsz, seq // ts),
        in_specs=[
            pl.BlockSpec((None, ts, ch), lambda bi, s: (bi, s, 0)),
            pl.BlockSpec((kw, ch), lambda bi, s: (0, 0)),
            vec, vec, vec,
        ],
        out_specs=pl.BlockSpec((None, ts, ch), lambda bi, s: (bi, s, 0)),
        out_shape=jax.ShapeDtypeStruct((bsz, seq, ch), BF16),
        scratch_shapes=[
            pltpu.VMEM((ts + CONV_HALO, ch), F32),
            pltpu.VMEM((ts, ch), F32),
        ],
        compiler_params=_cparams("parallel", "arbitrary"),
        name="conf_branch",
    )(c, dw, db, g, b)


def _merge_kernel(hr_ref, c_ref, wr_ref, wc_ref, bc_ref, gr_ref, gc_ref, o_ref, wrbf_ref, wcbf_ref):
    _cast_weights_once(pl.program_id(1), [(wr_ref, wrbf_ref), (wc_ref, wcbf_ref)])
    yr = _dot(hr_ref[...], wrbf_ref[...])
    yc = _dot(c_ref[...], wcbf_ref[...]) + bc_ref[...]
    o_ref[...] = (gr_ref[...].astype(F32) * yr + gc_ref[...].astype(F32) * yc).astype(o_ref.dtype)


def _merge(hr, cact, wr, wc, bc, gates, d_model, tm, tn):
    m, k = hr.shape
    goff = d_model // tn
    return pl.pallas_call(
        _merge_kernel,
        grid=(d_model // tn, m // tm),
        in_specs=[
            pl.BlockSpec((tm, k), lambda j, i: (i, 0)),
            pl.BlockSpec((tm, k), lambda j, i: (i, 0)),
            pl.BlockSpec((k, tn), lambda j, i: (0, j)),
            pl.BlockSpec((k, tn), lambda j, i: (0, j)),
            pl.BlockSpec((1, tn), lambda j, i: (0, j)),
            pl.BlockSpec((tm, tn), lambda j, i: (i, j)),
            pl.BlockSpec((tm, tn), lambda j, i: (i, j + goff)),
        ],
        out_specs=pl.BlockSpec((tm, tn), lambda j, i: (i, j)),
        out_shape=jax.ShapeDtypeStruct((m, d_model), BF16),
        scratch_shapes=[pltpu.VMEM((k, tn), BF16), pltpu.VMEM((k, tn), BF16)],
        compiler_params=_cparams("parallel", "arbitrary"),
        name="merge",
    )(hr, cact, wr, wc, bc, gates, gates)


def _mm_ln_kernel(a_ref, w_ref, res_ref, g_ref, b_ref, o_ref, obf_ref, wbf_ref, *, alpha):
    _cast_weights_once(pl.program_id(0), [(w_ref, wbf_ref)])
    y = _dot(a_ref[...], wbf_ref[...])
    z = _layer_norm_rows(alpha * res_ref[...] + y, g_ref[...], b_ref[...])
    o_ref[...] = z
    obf_ref[...] = z.astype(BF16)


def _mm_ln(a, w, res, g, b, alpha, tm):
    m, k = a.shape
    n = w.shape[1]
    vec = pl.BlockSpec((1, n), lambda i: (0, 0))
    return pl.pallas_call(
        functools.partial(_mm_ln_kernel, alpha=alpha),
        grid=(m // tm,),
        in_specs=[
            pl.BlockSpec((tm, k), lambda i: (i, 0)),
            pl.BlockSpec((k, n), lambda i: (0, 0), pipeline_mode=pl.Buffered(1)),
            pl.BlockSpec((tm, n), lambda i: (i, 0)),
            vec, vec,
        ],
        out_specs=[pl.BlockSpec((tm, n), lambda i: (i, 0)), pl.BlockSpec((tm, n), lambda i: (i, 0))],
        out_shape=[jax.ShapeDtypeStruct((m, n), F32), jax.ShapeDtypeStruct((m, n), BF16)],
        scratch_shapes=[pltpu.VMEM((k, n), BF16)],
        compiler_params=_cparams("arbitrary"),
        name="mm_ln",
    )(a, w, res, g, b)


def _add_ln_kernel(y_ref, res_ref, g_ref, b_ref, o_ref, *, alpha):
    o_ref[...] = _layer_norm_rows(alpha * res_ref[...] + y_ref[...], g_ref[...], b_ref[...])


def _add_ln(y, res, g, b, alpha, tm):
    m, n = y.shape
    vec = pl.BlockSpec((1, n), lambda i: (0, 0))
    return pl.pallas_call(
        functools.partial(_add_ln_kernel, alpha=alpha),
        grid=(m // tm,),
        in_specs=[pl.BlockSpec((tm, n), lambda i: (i, 0)), pl.BlockSpec((tm, n), lambda i: (i, 0)), vec, vec],
        out_specs=pl.BlockSpec((tm, n), lambda i: (i, 0)),
        out_shape=jax.ShapeDtypeStruct((m, n), F32),
        compiler_params=_cparams("parallel"),
        name="add_ln",
    )(y, res, g, b)


def _attn_kernel(q_ref, k_ref, v_ref, o_ref, *, heads):
    tq, d = q_ref.shape
    hd = d // heads
    scale = hd ** -0.5
    outs = []
    for h in range(heads):
        sl = slice(h * hd, (h + 1) * hd)
        s = lax.dot_general(q_ref[:, sl], k_ref[:, sl], (((1,), (1,)), ((), ())),
                            preferred_element_type=F32) * scale
        m = jnp.max(s, axis=-1, keepdims=True)
        p = jnp.exp(s - m)
        p = p / jnp.sum(p, axis=-1, keepdims=True)
        outs.append(_dot(p.astype(BF16), v_ref[:, sl]))
    o_ref[...] = jnp.concatenate(outs, axis=1).astype(o_ref.dtype)


def _attention(q, k, v, bsz, seq, mem_len, d_model, tq):
    nq = seq // tq
    return pl.pallas_call(
        functools.partial(_attn_kernel, heads=XA_HEADS),
        grid=(bsz, nq),
        in_specs=[
            pl.BlockSpec((tq, d_model), lambda b, i: (b * nq + i, 0)),
            pl.BlockSpec((mem_len, d_model), lambda b, i: (b, 0)),
            pl.BlockSpec((mem_len, d_model), lambda b, i: (b, 0)),
        ],
        out_specs=pl.BlockSpec((tq, d_model), lambda b, i: (b * nq + i, 0)),
        out_shape=jax.ShapeDtypeStruct((bsz * seq, d_model), BF16),
        compiler_params=_cparams("parallel", "parallel"),
        name="xattn",
    )(q, k, v)


def _extract_topk(s, k, pos, exact):
    rank = jnp.full(s.shape, float(k), F32)
    vals = []
    for r in range(k):
        m = jnp.max(s, axis=0, keepdims=True)
        hit = s == m
        if exact:
            first = jnp.min(jnp.where(hit, pos, jnp.inf), axis=0, keepdims=True)
            hit = pos == first
        rank = jnp.where(hit, float(r), rank)
        s = jnp.where(hit, -jnp.inf, s)
        vals.append(m)
    n_ranked = jnp.sum(jnp.where(rank < float(k), 1.0, 0.0), axis=0, keepdims=True)
    return vals, rank, n_ranked == float(k)


def _candidate_blocks(k):
    blocks = [("row", 0, k), ("row", 1, k // 2)]
    b = 0
    while 3 * (b + 1) <= k:
        hi = k // (b + 1)
        blocks.append(("col", b, -(-hi // SUBLANES) * SUBLANES, 2, hi))
        b += 1
    return blocks


def _choose_pairs(v1, v2, rank1, k, exact):
    tt = rank1.shape[1]
    v1_all = jnp.concatenate(v1, axis=0)
    v2_all = jnp.concatenate(v2, axis=0)
    e1_all = jnp.exp(v1_all - v1[0])
    e2_all = jnp.exp(v2_all - v2[0])
    cands, prods, poss = [], [], []
    for blk in _candidate_blocks(k):
        if blk[0] == "row":
            _, a, n = blk
            cands.append(v1[a] + v2_all[0:n])
            prods.append(e1_all[a:a + 1] * e2_all[0:n])
            poss.append((a * k + lax.broadcasted_iota(jnp.int32, (n, tt), 0)).astype(F32))
        else:
            _, b, n, lo, hi = blk
            a_col = lax.broadcasted_iota(jnp.int32, (n, tt), 0)
            valid = jnp.logical_and(a_col >= lo, a_col < hi)
            cands.append(jnp.where(valid, v1_all[0:n] + v2[b], -jnp.inf))
            prods.append(e1_all[0:n] * e2_all[b:b + 1])
            poss.append(jnp.where(valid, (a_col * k + b).astype(F32), jnp.inf))
    _, rankc, okc = _extract_topk(jnp.concatenate(cands, axis=0), k, jnp.concatenate(poss, axis=0), exact)
    sel = jnp.where(rankc < float(k), 1.0, 0.0)
    z = jnp.sum(sel * jnp.concatenate(prods, axis=0), axis=0, keepdims=True)
    cnt = jnp.zeros((k, tt), F32)
    a_col = lax.broadcasted_iota(jnp.int32, (k, tt), 0)
    row0 = 0
    for blk in _candidate_blocks(k):
        n = blk[2]
        part = sel[row0:row0 + n]
        row0 += n
        if blk[0] == "row":
            cnt = cnt + jnp.where(a_col == blk[1], jnp.sum(part, axis=0, keepdims=True), 0.0)
        elif n == k:
            cnt = cnt + part
        else:
            cnt = cnt + jnp.concatenate([part, jnp.zeros((k - n, part.shape[1]), F32)], axis=0)
    c1 = jnp.zeros_like(rank1)
    for a in range(k):
        c1 = jnp.where(rank1 == float(a), cnt[a:a + 1], c1)
    return c1, z, okc


def _route_head(s1, s2, k, exact):
    key_pos = lax.broadcasted_iota(jnp.int32, s1.shape, 0).astype(F32)
    v1, rank1, ok1 = _extract_topk(s1, k, key_pos, exact)
    v2, rank2, ok2 = _extract_topk(s2, k, key_pos, exact)
    c1, z, _ = _choose_pairs(v1, v2, rank1, k, True)
    ok = jnp.logical_and(ok1, ok2)
    return c1, jnp.exp(s1 - v1[0]), rank2, jnp.exp(s2 - v2[0]) / z, ok


def _peer_route_kernel(x_ref, wq_ref, keys_ref, c1_ref, e1_ref, r2_ref, e2_ref, wqbf_ref, *, heads, topk):
    n_keys, half = keys_ref.shape[1], keys_ref.shape[2]
    _cast_weights_once(pl.program_id(0), [(wq_ref, wqbf_ref)])
    q = _dot(x_ref[...], wqbf_ref[...])
    nt = (((1,), (1,)), ((), ()))

    def route_all(exact):
        ok = None
        for h in range(heads):
            base = h * 2 * half
            s1 = lax.dot_general(keys_ref[0], q[:, base:base + half], nt, preferred_element_type=F32)
            s2 = lax.dot_general(keys_ref[1], q[:, base + half:base + 2 * half], nt, preferred_element_type=F32)
            c1, e1, rank2, e2, ok_h = _route_head(s1, s2, topk, exact)
            c1_ref[h] = _dup_bf16_words(c1)
            e1_ref[h] = _dup_bf16_words(e1)
            r2_ref[h] = rank2.astype(BF16)
            e2_ref[h] = e2.astype(BF16)
            ok = ok_h if ok is None else jnp.logical_and(ok, ok_h)
        return ok

    ok = route_all(exact=False)
    n_bad = jnp.sum(jnp.where(ok, 0.0, 1.0))

    @pl.when(n_bad > 0.0)
    def _():
        route_all(exact=True)


def _dup_bf16_words(v):
    b = lax.bitcast_convert_type(v.astype(BF16).astype(F32), jnp.uint32)
    return b | (b >> 16)


def _peer_route(x_bf, wq, keys, heads, tt):
    t, d = x_bf.shape
    n_keys = keys.shape[1]
    out_w = jax.ShapeDtypeStruct((heads, n_keys, t), jnp.uint32)
    out_b = jax.ShapeDtypeStruct((heads, n_keys, t), BF16)
    ospec = pl.BlockSpec((heads, n_keys, tt), lambda i: (0, 0, i))
    return pl.pallas_call(
        functools.partial(_peer_route_kernel, heads=heads, topk=PEER_TOPK),
        grid=(t // tt,),
        in_specs=[
            pl.BlockSpec((tt, d), lambda i: (i, 0)),
            pl.BlockSpec(wq.shape, lambda i: (0, 0)),
            pl.BlockSpec(keys.shape, lambda i: (0, 0, 0)),
        ],
        out_specs=[ospec, ospec, ospec, ospec],
        out_shape=[out_w, out_w, out_b, out_b],
        scratch_shapes=[pltpu.VMEM(wq.shape, BF16)],
        compiler_params=_cparams("arbitrary"),
        name="peer_route",
    )(x_bf, wq, keys)


def _peer_dense_kernel(xn_ref, un_ref, vt_ref, c1_ref, e1_ref, r2_ref, e2_ref, o_ref, acc_ref, ht_ref, act_ref, *,
                       heads, n_e):
    s_idx = pl.program_id(0)
    e_gate = jnp.maximum(s_idx - 1, 0) % n_e
    e_down = jnp.maximum(s_idx - 2, 0) % n_e
    ec = un_ref.shape[0]
    tt = xn_ref.shape[0]
    n_keys = r2_ref.shape[1]
    rows_per_step = ec // n_keys

    @pl.when(s_idx == 0)
    def _():
        ht_ref[...] = jnp.zeros_like(ht_ref)
        act_ref[...] = jnp.zeros_like(act_ref)

    @pl.when(e_down == 0)
    def _():
        acc_ref[...] = jnp.zeros_like(acc_ref)

    acc_ref[...] += _dot(vt_ref[...], act_ref[...])
    ht = ht_ref[...]
    ht_ref[...] = lax.dot_general(un_ref[...], xn_ref[...], (((1,), (1,)), ((), ())), preferred_element_type=F32)

    def row_bf16(ref, h, i):
        words = jnp.broadcast_to(ref[h, pl.ds(i, 1), :], (n_keys // 2, tt))
        return pltpu.bitcast(words, BF16)

    for ii in range(rows_per_step):
        i = e_gate * rows_per_step + ii
        g = None
        for h in range(heads):
            sel = jnp.where(r2_ref[h] < row_bf16(c1_ref, h, i), e2_ref[h], jnp.zeros((), BF16))
            term = sel * row_bf16(e1_ref, h, i)
            g = term if g is None else g + term
        rows = slice(ii * n_keys, (ii + 1) * n_keys)
        act_ref[rows, :] = jax.nn.gelu(ht[rows, :]).astype(BF16) * g

    @pl.when(jnp.logical_and(e_down == n_e - 1, s_idx > 1))
    def _():
        o_ref[...] = acc_ref[...].T


def _peer_dense(x_bf, u_bf, vt_bf, c1, e1, r2, e2, tt, ec):
    t, d = x_bf.shape
    n_exp = u_bf.shape[0]
    heads, n_keys, _ = c1.shape
    n_t, n_e = t // tt, n_exp // ec
    n_chunks = n_t * n_e
    up = lambda s: jnp.minimum(s, n_chunks - 1)
    gate = lambda s: jnp.clip(s - 1, 0, n_chunks - 1)
    down = lambda s: jnp.maximum(s - 2, 0)
    rspec = pl.BlockSpec((heads, n_keys, tt), lambda s: (0, 0, gate(s) // n_e))
    return pl.pallas_call(
        functools.partial(_peer_dense_kernel, heads=heads, n_e=n_e),
        grid=(n_chunks + 2,),
        in_specs=[
            pl.BlockSpec((tt, d), lambda s: (up(s) // n_e, 0)),
            pl.BlockSpec((ec, d), lambda s: (up(s) % n_e, 0)),
            pl.BlockSpec((None, d, ec), lambda s: (down(s) % n_e, 0, 0)),
            rspec, rspec, rspec, rspec,
        ],
        out_specs=pl.BlockSpec((tt, d), lambda s: (down(s) // n_e, 0)),
        out_shape=jax.ShapeDtypeStruct((t, d), F32),
        scratch_shapes=[pltpu.VMEM((d, tt), F32), pltpu.VMEM((ec, tt), F32), pltpu.VMEM((ec, tt), BF16)],
        compiler_params=_cparams("arbitrary"),
        name="peer_dense",
    )(x_bf, u_bf, vt_bf, c1, e1, r2, e2)


def _transpose_cast_kernel(v_ref, o_ref):
    o_ref[...] = v_ref[...].T.astype(o_ref.dtype)


def _transpose_cast(v, te):
    n_exp, d = v.shape
    return pl.pallas_call(
        _transpose_cast_kernel,
        grid=(n_exp // te,),
        in_specs=[pl.BlockSpec((te, d), lambda e: (e, 0))],
        out_specs=pl.BlockSpec((None, d, te), lambda e: (e, 0, 0)),
        out_shape=jax.ShapeDtypeStruct((n_exp // te, d, te), BF16),
        compiler_params=_cparams("parallel"),
        name="transpose_cast",
    )(v)


def _tile(n, pref):
    t = min(n, pref)
    assert n % t == 0
    return t


def _layer(x, mem, w_in, b_in, rnn_conv_w, rnn_conv_b, rnn_w_a, rnn_b_a, rnn_w_x, rnn_b_x, rnn_lambda,
           w_rnn_out, conf_dw_w, conf_dw_b, conf_ln_g, conf_ln_b, w_conf_out, b_conf_out, w_mix_out,
           ln1_g, ln1_b, xa_w_q, xa_w_k, xa_w_v, xa_w_o, ln2_g, ln2_b,
           peer_w_q, peer_sub_keys, peer_u, peer_v, ln3_g, ln3_b, alpha):
    bsz, seq, d = x.shape
    t = bsz * seq
    mem_len = mem.shape[1]
    d_rnn = rnn_conv_w.shape[1]
    d_conv = conf_dw_w.shape[1]
    row = lambda v: v.reshape(1, -1)

    tm = _tile(t, 1024)
    tn = _tile(d, 1024)
    x2d = x.reshape(t, d)
    x_bf = x2d.astype(BF16)
    b_in2 = row(b_in)
    tn2 = _tile(d, 512)

    xr = _proj(x_bf, w_in, b_in2, 0, d_rnn, "none", F32, tm, tn)
    gg = _proj(x_bf, w_in, b_in2, d_rnn, d_rnn, "gelu", BF16, tm, tn)
    cglu = _glu(x_bf, w_in, b_in2, 2 * d_rnn, 2 * d_rnn + d_conv, d_conv, tm, tn2)
    mgates = _proj(x_bf, w_in, b_in2, 2 * d_rnn + 2 * d_conv, 2 * d, "sigmoid", BF16, tm, tn)

    hr = _rnn_branch(xr.reshape(bsz, seq, d_rnn), rnn_conv_w, row(rnn_conv_b),
                     rnn_w_a.astype(BF16), rnn_w_x.astype(BF16), row(rnn_b_a), row(rnn_b_x),
                     row(rnn_lambda), gg.reshape(bsz, seq, d_rnn),
                     _tile(seq, 512), _tile(d_rnn, 512))
    cact = _conf_branch(cglu.reshape(bsz, seq, d_conv), conf_dw_w, row(conf_dw_b),
                        row(conf_ln_g), row(conf_ln_b), _tile(seq, 256))

    merged = _merge(hr.reshape(t, d_rnn), cact.reshape(t, d_conv), w_rnn_out,
                    w_conf_out, row(b_conf_out), mgates, d, tm, tn2)
    tl = _tile(t, 512)
    x1, x1_bf = _mm_ln(merged, w_mix_out, x2d, row(ln1_g), row(ln1_b), alpha, tl)

    zero_d = jnp.zeros((1, d), F32)
    q = _proj(x1_bf, xa_w_q, zero_d, 0, d, "none", BF16, tm, tn)
    mem_bf = mem.reshape(bsz * mem_len, d).astype(BF16)
    tmem = _tile(bsz * mem_len, 1024)
    k = _proj(mem_bf, xa_w_k, zero_d, 0, d, "none", BF16, tmem, tn)
    v = _proj(mem_bf, xa_w_v, zero_d, 0, d, "none", BF16, tmem, tn)
    o = _attention(q, k, v, bsz, seq, mem_len, d, _tile(seq, 512))
    x2, x2_bf = _mm_ln(o, xa_w_o, x1, row(ln2_g), row(ln2_b), alpha, tl)

    heads = peer_w_q.shape[1] // (2 * peer_sub_keys.shape[2])
    c1, e1, r2, e2 = _peer_route(x2_bf, peer_w_q, peer_sub_keys, heads, _tile(t, 256))
    n_exp = peer_u.shape[0]
    ec = _tile(n_exp, 1024)
    vt_bf = _transpose_cast(peer_v, ec)
    ff = _peer_dense(x2_bf, peer_u.astype(BF16), vt_bf, c1, e1, r2, e2, _tile(t, 512), ec)
    x3 = _add_ln(ff, x2, row(ln3_g), row(ln3_b), alpha, tl)
    return x3.reshape(bsz, seq, d)


def kernel(x, mem, w_in, b_in, rnn_conv_w, rnn_conv_b, rnn_w_a, rnn_b_a, rnn_w_x, rnn_b_x, rnn_lambda, w_rnn_out, conf_dw_w, conf_dw_b, conf_ln_g, conf_ln_b, w_conf_out, b_conf_out, w_mix_out, ln1_g, ln1_b, xa_w_q, xa_w_k, xa_w_v, xa_w_o, ln2_g, ln2_b, peer_w_q, peer_sub_keys, peer_u, peer_v, ln3_g, ln3_b):
    depth = w_in.shape[0]
    alpha = (2 * depth) ** 0.25
    params = (w_in, b_in, rnn_conv_w, rnn_conv_b, rnn_w_a, rnn_b_a, rnn_w_x, rnn_b_x, rnn_lambda, w_rnn_out,
              conf_dw_w, conf_dw_b, conf_ln_g, conf_ln_b, w_conf_out, b_conf_out, w_mix_out, ln1_g, ln1_b,
              xa_w_q, xa_w_k, xa_w_v, xa_w_o, ln2_g, ln2_b, peer_w_q, peer_sub_keys, peer_u, peer_v,
              ln3_g, ln3_b)
    for l in range(depth):
        x = _layer(x, mem, *[p[l] for p in params], alpha)
    return x
```

```python
import functools
import math

import jax
import jax.numpy as jnp
from jax import lax
from jax.experimental import pallas as pl
from jax.experimental.pallas import tpu as pltpu

F32 = jnp.float32
BF16 = jnp.bfloat16

LRU_C = 8.0
LN_EPS = 1e-5
XA_HEADS = 4
PEER_TOPK = 16
VMEM_LIMIT_BYTES = 56 * 1024 * 1024
LANES = 128
SUBLANES = 8


def _cparams(*sem):
    return pltpu.CompilerParams(dimension_semantics=sem, vmem_limit_bytes=VMEM_LIMIT_BYTES)


def _layer_norm_rows(z, g, b):
    mu = jnp.mean(z, axis=-1, keepdims=True)
    zc = z - mu
    var = jnp.mean(zc * zc, axis=-1, keepdims=True)
    return zc * lax.rsqrt(var + LN_EPS) * g + b


def _dot(a, b):
    return jnp.dot(a, b, preferred_element_type=F32)


def _cast_weights_once(step, pairs):
    @pl.when(step == 0)
    def _():
        for w_ref, wbf_ref in pairs:
            wbf_ref[...] = w_ref[...].astype(BF16)


def _proj_kernel(a_ref, w_ref, b_ref, o_ref, wbf_ref, *, act):
    _cast_weights_once(pl.program_id(1), [(w_ref, wbf_ref)])
    y = _dot(a_ref[...], wbf_ref[...]) + b_ref[...]
    if act == "gelu":
        y = jax.nn.gelu(y)
    elif act == "sigmoid":
        y = jax.nn.sigmoid(y)
    o_ref[...] = y.astype(o_ref.dtype)


def _proj_cast_kernel(a_ref, w_ref, b_ref, side_ref, o_ref, side_o_ref, wbf_ref, *, act):
    side_o_ref[...] = side_ref[...].astype(BF16)
    _proj_kernel(a_ref, w_ref, b_ref, o_ref, wbf_ref, act=act)


def _proj(a, w, b, col0, ncols, act, out_dtype, tm, tn, side=None):
    m, k = a.shape
    c0 = col0 // tn
    n_m = m // tm
    in_specs = [
        pl.BlockSpec((tm, k), lambda j, i: (i, 0)),
        pl.BlockSpec((k, tn), lambda j, i: (0, j + c0)),
        pl.BlockSpec((1, tn), lambda j, i: (0, j + c0)),
    ]
    out_specs = pl.BlockSpec((tm, tn), lambda j, i: (i, j))
    out_shape = jax.ShapeDtypeStruct((m, ncols), out_dtype)
    kern, args = functools.partial(_proj_kernel, act=act), (a, w, b)
    if side is not None:
        rs = side.shape[0] // ((ncols // tn) * n_m)
        side_spec = pl.BlockSpec((rs, side.shape[1]), lambda j, i: (j * n_m + i, 0))
        in_specs.append(side_spec)
        out_specs = [out_specs, side_spec]
        out_shape = [out_shape, jax.ShapeDtypeStruct(side.shape, BF16)]
        kern, args = functools.partial(_proj_cast_kernel, act=act), (a, w, b, side)
    return pl.pallas_call(
        kern,
        grid=(ncols // tn, n_m),
        in_specs=in_specs,
        out_specs=out_specs,
        out_shape=out_shape,
        scratch_shapes=[pltpu.VMEM((k, tn), BF16)],
        compiler_params=_cparams("parallel", "arbitrary"),
        name="proj_" + act,
    )(*args)


def _glu_kernel(a_ref, w1_ref, w2_ref, b1_ref, b2_ref, o_ref, w1bf_ref, w2bf_ref):
    _cast_weights_once(pl.program_id(1), [(w1_ref, w1bf_ref), (w2_ref, w2bf_ref)])
    a = a_ref[...]
    y1 = _dot(a, w1bf_ref[...]) + b1_ref[...]
    y2 = _dot(a, w2bf_ref[...]) + b2_ref[...]
    o_ref[...] = (y1 * jax.nn.sigmoid(y2)).astype(o_ref.dtype)


def _glu_transpose_kernel(a_ref, w1_ref, w2_ref, b1_ref, b2_ref, side_ref, o_ref, side_o_ref, w1bf_ref, w2bf_ref):
    side_o_ref[...] = side_ref[...].T.astype(BF16)
    _glu_kernel(a_ref, w1_ref, w2_ref, b1_ref, b2_ref, o_ref, w1bf_ref, w2bf_ref)


def _glu(a, w, b, col1, col2, ncols, tm, tn, side=None, side_chunk=None):
    m, k = a.shape
    c1, c2 = col1 // tn, col2 // tn
    n_m = m // tm
    in_specs = [
        pl.BlockSpec((tm, k), lambda j, i: (i, 0)),
        pl.BlockSpec((k, tn), lambda j, i: (0, j + c1)),
        pl.BlockSpec((k, tn), lambda j, i: (0, j + c2)),
        pl.BlockSpec((1, tn), lambda j, i: (0, j + c1)),
        pl.BlockSpec((1, tn), lambda j, i: (0, j + c2)),
    ]
    out_specs = pl.BlockSpec((tm, tn), lambda j, i: (i, j))
    out_shape = jax.ShapeDtypeStruct((m, ncols), F32)
    kern, args = _glu_kernel, (a, w, w, b, b)
    if side is not None:
        rows, cols = side.shape
        rs = rows // ((ncols // tn) * n_m)
        per_chunk = side_chunk // rs
        in_specs.append(pl.BlockSpec((rs, cols), lambda j, i: (j * n_m + i, 0)))
        out_specs = [out_specs, pl.BlockSpec((None, cols, rs), lambda j, i: ((j * n_m + i) // per_chunk, 0,
                                                                             (j * n_m + i) % per_chunk))]
        out_shape = [out_shape, jax.ShapeDtypeStruct((rows // side_chunk, cols, side_chunk), BF16)]
        kern, args = _glu_transpose_kernel, (a, w, w, b, b, side)
    return pl.pallas_call(
        kern,
        grid=(ncols // tn, n_m),
        in_specs=in_specs,
        out_specs=out_specs,
        out_shape=out_shape,
        scratch_shapes=[pltpu.VMEM((k, tn), BF16), pltpu.VMEM((k, tn), BF16)],
        compiler_params=_cparams("parallel", "arbitrary"),
        name="proj_glu",
    )(*args)


def _rnn_kernel(xr_ref, cw_ref, cb_ref, wa_ref, wx_ref, ba_ref, bx_ref, lam_ref, gg_ref, o_ref,
                ext_ref, a_ref, u_ref, h_ref):
    ts, ct = xr_ref.shape
    kw = cw_ref.shape[0]
    s_idx = pl.program_id(2)

    @pl.when(s_idx == 0)
    def _():
        ext_ref[0:SUBLANES, :] = jnp.zeros((SUBLANES, ct), F32)
        h_ref[...] = jnp.zeros_like(h_ref)

    ext_ref[SUBLANES:SUBLANES + ts, :] = xr_ref[...]
    y = jnp.zeros((ts, ct), F32) + cb_ref[...]
    ext = ext_ref[...]
    for k in range(kw):
        off = SUBLANES - (kw - 1) + k
        z = ext if off % SUBLANES == 0 else pltpu.roll(ext, ts + SUBLANES - off % SUBLANES, 0)
        base = off - off % SUBLANES
        y = y + cw_ref[k:k + 1, :] * z[base:base + ts, :]
    ext_ref[0:SUBLANES, :] = ext_ref[ts:ts + SUBLANES, :]

    nblk = ct // LANES
    r_parts, i_parts = [], []
    for blk in range(nblk):
        yb = y[:, blk * LANES:(blk + 1) * LANES].astype(BF16)
        r_parts.append(_dot(yb, wa_ref[blk]))
        i_parts.append(_dot(yb, wx_ref[blk]))
    r = jax.nn.sigmoid(jnp.concatenate(r_parts, axis=1) + ba_ref[...])
    ig = jax.nn.sigmoid(jnp.concatenate(i_parts, axis=1) + bx_ref[...])
    lam = lam_ref[...]
    sp = jnp.maximum(-lam, 0.0) + jnp.log1p(jnp.exp(-jnp.abs(lam)))
    log_a = (-LRU_C * r) * sp
    a = jnp.exp(log_a)
    u = jnp.sqrt(1.0 - jnp.exp(2.0 * log_a)) * (ig * y)
    a_ref[...] = a
    u_ref[...] = u

    row = lax.broadcasted_iota(jnp.int32, (SUBLANES, ct), 0)

    def body(g, h):
        r0 = pl.multiple_of(g * SUBLANES, SUBLANES)
        av = a_ref[pl.ds(r0, SUBLANES), :]
        uv = u_ref[pl.ds(r0, SUBLANES), :]
        for d in (1, 2, 4):
            a_sh = jnp.where(row >= d, pltpu.roll(av, d, 0), 1.0)
            u_sh = jnp.where(row >= d, pltpu.roll(uv, d, 0), 0.0)
            uv = av * u_sh + uv
            av = av * a_sh
        hh = uv + av * h
        u_ref[pl.ds(r0, SUBLANES), :] = hh
        return hh[SUBLANES - 1:SUBLANES, :]

    h_last = lax.fori_loop(0, ts // SUBLANES, body, h_ref[...], unroll=4)
    h_ref[...] = h_last
    o_ref[...] = (u_ref[...] * gg_ref[...].astype(F32)).astype(o_ref.dtype)


def _rnn_branch(xr, cw, cb, wa, wx, ba, bx, lam, gg, ts, ct):
    bsz, seq, c = xr.shape
    kw = cw.shape[0]
    assert kw - 1 <= SUBLANES
    nb = ct // LANES
    vec = pl.BlockSpec((1, ct), lambda b, j, s: (0, j))
    return pl.pallas_call(
        _rnn_kernel,
        grid=(bsz, c // ct, seq // ts),
        in_specs=[
            pl.BlockSpec((None, ts, ct), lambda b, j, s: (b, s, j)),
            pl.BlockSpec((kw, ct), lambda b, j, s: (0, j)),
            vec,
            pl.BlockSpec((nb, LANES, LANES), lambda b, j, s: (j, 0, 0)),
            pl.BlockSpec((nb, LANES, LANES), lambda b, j, s: (j, 0, 0)),
            vec, vec, vec,
            pl.BlockSpec((None, ts, ct), lambda b, j, s: (b, s, j)),
        ],
        out_specs=pl.BlockSpec((None, ts, ct), lambda b, j, s: (b, s, j)),
        out_shape=jax.ShapeDtypeStruct((bsz, seq, c), BF16),
        scratch_shapes=[
            pltpu.VMEM((ts + SUBLANES, ct), F32),
            pltpu.VMEM((ts, ct), F32),
            pltpu.VMEM((ts, ct), F32),
            pltpu.VMEM((1, ct), F32),
        ],
        compiler_params=_cparams("parallel", "parallel", "arbitrary"),
        name="rnn_branch",
    )(xr, cw, cb, wa, wx, ba, bx, lam, gg)


CONV_HALO = 32
CONV_LANE_CHUNK = 128
CONV_ROW_BLOCK = 128


def _conf_kernel(c_ref, dw_ref, db_ref, g_ref, b_ref, o_ref, ext_ref, y_ref):
    ts, ch = c_ref.shape
    kw = dw_ref.shape[0]
    s_idx = pl.program_id(1)

    @pl.when(s_idx == 0)
    def _():
        ext_ref[0:CONV_HALO, :] = jnp.zeros((CONV_HALO, ch), F32)

    ext_ref[CONV_HALO:CONV_HALO + ts, :] = c_ref[...]
    off0 = CONV_HALO - (kw - 1)
    rb_rows = min(CONV_ROW_BLOCK, ts)
    win_rows = CONV_HALO + rb_rows
    lc = min(CONV_LANE_CHUNK, ch)
    n_lane = ch // lc

    def body(it, carry):
        rb = it // n_lane
        lb = it % n_lane
        r0 = pl.multiple_of(rb * rb_rows, rb_rows)
        l0 = pl.multiple_of(lb * lc, lc)
        win = ext_ref[pl.ds(r0, win_rows), pl.ds(l0, lc)]
        acc = jnp.zeros((rb_rows, lc), F32) + db_ref[:, pl.ds(l0, lc)]
        for r in range(SUBLANES):
            z = win if r == 0 else pltpu.roll(win, win_rows - r, 0)
            for q in range(win_rows // SUBLANES):
                k = q * SUBLANES + r - off0
                if 0 <= k < kw:
                    acc = acc + dw_ref[k:k + 1, pl.ds(l0, lc)] * z[q * SUBLANES:q * SUBLANES + rb_rows, :]
        y_ref[pl.ds(r0, rb_rows), pl.ds(l0, lc)] = acc
        return carry

    lax.fori_loop(0, (ts // rb_rows) * n_lane, body, 0)
    ext_ref[0:CONV_HALO, :] = ext_ref[ts:ts + CONV_HALO, :]
    yn = _layer_norm_rows(y_ref[...], g_ref[...], b_ref[...])
    o_ref[...] = (yn * jax.nn.sigmoid(yn)).astype(o_ref.dtype)


def _conf_branch(c, dw, db, g, b, ts):
    bsz, seq, ch = c.shape
    kw = dw.shape[0]
    assert kw - 1 <= CONV_HALO
    vec = pl.BlockSpec((1, ch), lambda bi, s: (0, 0))
    return pl.pallas_call(
        _conf_kernel,
        grid=(bsz, seq // ts),
        in_specs=[
            pl.BlockSpec((None, ts, ch), lambda bi, s: (bi, s, 0)),
            pl.BlockSpec((kw, ch), lambda bi, s: (0, 0)),
            vec, vec, vec,
        ],
        out_specs=pl.BlockSpec((None, ts, ch), lambda bi, s: (bi, s, 0)),
        out_shape=jax.ShapeDtypeStruct((bsz, seq, ch), BF16),
        scratch_shapes=[
            pltpu.VMEM((ts + CONV_HALO, ch), F32),
            pltpu.VMEM((ts, ch), F32),
        ],
        compiler_params=_cparams("parallel", "arbitrary"),
        name="conf_branch",
    )(c, dw, db, g, b)


def _merge_kernel(hr_ref, c_ref, wr_ref, wc_ref, bc_ref, gr_ref, gc_ref, o_ref, wrbf_ref, wcbf_ref):
    _cast_weights_once(pl.program_id(1), [(wr_ref, wrbf_ref), (wc_ref, wcbf_ref)])
    yr = _dot(hr_ref[...], wrbf_ref[...])
    yc = _dot(c_ref[...], wcbf_ref[...]) + bc_ref[...]
    o_ref[...] = (gr_ref[...].astype(F32) * yr + gc_ref[...].astype(F32) * yc).astype(o_ref.dtype)


def _merge(hr, cact, wr, wc, bc, gates, d_model, tm, tn):
    m, k = hr.shape
    goff = d_model // tn
    return pl.pallas_call(
        _merge_kernel,
        grid=(d_model // tn, m // tm),
        in_specs=[
            pl.BlockSpec((tm, k), lambda j, i: (i, 0)),
            pl.BlockSpec((tm, k), lambda j, i: (i, 0)),
            pl.BlockSpec((k, tn), lambda j, i: (0, j)),
            pl.BlockSpec((k, tn), lambda j, i: (0, j)),
            pl.BlockSpec((1, tn), lambda j, i: (0, j)),
            pl.BlockSpec((tm, tn), lambda j, i: (i, j)),
            pl.BlockSpec((tm, tn), lambda j, i: (i, j + goff)),
        ],
        out_specs=pl.BlockSpec((tm, tn), lambda j, i: (i, j)),
        out_shape=jax.ShapeDtypeStruct((m, d_model), BF16),
        scratch_shapes=[pltpu.VMEM((k, tn), BF16), pltpu.VMEM((k, tn), BF16)],
        compiler_params=_cparams("parallel", "arbitrary"),
        name="merge",
    )(hr, cact, wr, wc, bc, gates, gates)


def _mm_ln_kernel(a_ref, w_ref, res_ref, g_ref, b_ref, o_ref, obf_ref, wbf_ref, *, alpha):
    _cast_weights_once(pl.program_id(0), [(w_ref, wbf_ref)])
    y = _dot(a_ref[...], wbf_ref[...])
    z = _layer_norm_rows(alpha * res_ref[...] + y, g_ref[...], b_ref[...])
    o_ref[...] = z
    obf_ref[...] = z.astype(BF16)


def _mm_ln(a, w, res, g, b, alpha, tm):
    m, k = a.shape
    n = w.shape[1]
    vec = pl.BlockSpec((1, n), lambda i: (0, 0))
    return pl.pallas_call(
        functools.partial(_mm_ln_kernel, alpha=alpha),
        grid=(m // tm,),
        in_specs=[
            pl.BlockSpec((tm, k), lambda i: (i, 0)),
            pl.BlockSpec((k, n), lambda i: (0, 0), pipeline_mode=pl.Buffered(1)),
            pl.BlockSpec((tm, n), lambda i: (i, 0)),
            vec, vec,
        ],
        out_specs=[pl.BlockSpec((tm, n), lambda i: (i, 0)), pl.BlockSpec((tm, n), lambda i: (i, 0))],
        out_shape=[jax.ShapeDtypeStruct((m, n), F32), jax.ShapeDtypeStruct((m, n), BF16)],
        scratch_shapes=[pltpu.VMEM((k, n), BF16)],
        compiler_params=_cparams("arbitrary"),
        name="mm_ln",
    )(a, w, res, g, b)


def _add_ln_kernel(y_ref, res_ref, g_ref, b_ref, o_ref, *, alpha):
    o_ref[...] = _layer_norm_rows(alpha * res_ref[...] + y_ref[...], g_ref[...], b_ref[...])


def _add_ln(y, res, g, b, alpha, tm):
    m, n = y.shape
    vec = pl.BlockSpec((1, n), lambda i: (0, 0))
    return pl.pallas_call(
        functools.partial(_add_ln_kernel, alpha=alpha),
        grid=(m // tm,),
        in_specs=[pl.BlockSpec((tm, n), lambda i: (i, 0)), pl.BlockSpec((tm, n), lambda i: (i, 0)), vec, vec],
        out_specs=pl.BlockSpec((tm, n), lambda i: (i, 0)),
        out_shape=jax.ShapeDtypeStruct((m, n), F32),
        compiler_params=_cparams("parallel"),
        name="add_ln",
    )(y, res, g, b)


def _attn_kernel(q_ref, k_ref, v_ref, o_ref, *, heads):
    tq, d = q_ref.shape
    hd = d // heads
    scale = hd ** -0.5
    outs = []
    for h in range(heads):
        sl = slice(h * hd, (h + 1) * hd)
        s = lax.dot_general(q_ref[:, sl], k_ref[:, sl], (((1,), (1,)), ((), ())),
                            preferred_element_type=F32) * scale
        m = jnp.max(s, axis=-1, keepdims=True)
        p = jnp.exp(s - m)
        p = p / jnp.sum(p, axis=-1, keepdims=True)
        outs.append(_dot(p.astype(BF16), v_ref[:, sl]))
    o_ref[...] = jnp.concatenate(outs, axis=1).astype(o_ref.dtype)


def _attention(q, k, v, bsz, seq, mem_len, d_model, tq):
    nq = seq // tq
    return pl.pallas_call(
        functools.partial(_attn_kernel, heads=XA_HEADS),
        grid=(bsz, nq),
        in_specs=[
            pl.BlockSpec((tq, d_model), lambda b, i: (b * nq + i, 0)),
            pl.BlockSpec((mem_len, d_model), lambda b, i: (b, 0)),
            pl.BlockSpec((mem_len, d_model), lambda b, i: (b, 0)),
        ],
        out_specs=pl.BlockSpec((tq, d_model), lambda b, i: (b * nq + i, 0)),
        out_shape=jax.ShapeDtypeStruct((bsz * seq, d_model), BF16),
        compiler_params=_cparams("parallel", "parallel"),
        name="xattn",
    )(q, k, v)


def _extract_topk(s, k, pos, exact):
    rank = jnp.full(s.shape, float(k), F32)
    vals = []
    for r in range(k):
        m = jnp.max(s, axis=0, keepdims=True)
        hit = s == m
        if exact:
            first = jnp.min(jnp.where(hit, pos, jnp.inf), axis=0, keepdims=True)
            hit = pos == first
        rank = jnp.where(hit, float(r), rank)
        s = jnp.where(hit, -jnp.inf, s)
        vals.append(m)
    n_ranked = jnp.sum(jnp.where(rank < float(k), 1.0, 0.0), axis=0, keepdims=True)
    return vals, rank, n_ranked == float(k)


def _candidate_blocks(k):
    blocks = [("row", 0, k), ("row", 1, k // 2)]
    b = 0
    while 3 * (b + 1) <= k:
        hi = k // (b + 1)
        blocks.append(("col", b, -(-hi // SUBLANES) * SUBLANES, 2, hi))
        b += 1
    return blocks


def _choose_pairs(v1, v2, rank1, k, exact):
    tt = rank1.shape[1]
    v1_all = jnp.concatenate(v1, axis=0)
    v2_all = jnp.concatenate(v2, axis=0)
    e1_all = jnp.exp(v1_all - v1[0])
    e2_all = jnp.exp(v2_all - v2[0])
    cands, prods, poss = [], [], []
    for blk in _candidate_blocks(k):
        if blk[0] == "row":
            _, a, n = blk
            cands.append(v1[a] + v2_all[0:n])
            prods.append(e1_all[a:a + 1] * e2_all[0:n])
            poss.append((a * k + lax.broadcasted_iota(jnp.int32, (n, tt), 0)).astype(F32))
        else:
            _, b, n, lo, hi = blk
            a_col = lax.broadcasted_iota(jnp.int32, (n, tt), 0)
            valid = jnp.logical_and(a_col >= lo, a_col < hi)
            cands.append(jnp.where(valid, v1_all[0:n] + v2[b], -jnp.inf))
            prods.append(e1_all[0:n] * e2_all[b:b + 1])
            poss.append(jnp.where(valid, (a_col * k + b).astype(F32), jnp.inf))
    _, rankc, okc = _extract_topk(jnp.concatenate(cands, axis=0), k, jnp.concatenate(poss, axis=0), exact)
    sel = jnp.where(rankc < float(k), 1.0, 0.0)
    z = jnp.sum(sel * jnp.concatenate(prods, axis=0), axis=0, keepdims=True)
    cnt = jnp.zeros((k, tt), F32)
    a_col = lax.broadcasted_iota(jnp.int32, (k, tt), 0)
    row0 = 0
    for blk in _candidate_blocks(k):
        n = blk[2]
        part = sel[row0:row0 + n]
        row0 += n
        if blk[0] == "row":
            cnt = cnt + jnp.where(a_col == blk[1], jnp.sum(part, axis=0, keepdims=True), 0.0)
        elif n == k:
            cnt = cnt + part
        else:
            cnt = cnt + jnp.concatenate([part, jnp.zeros((k - n, part.shape[1]), F32)], axis=0)
    c1 = jnp.zeros_like(rank1)
    for a in range(k):
        c1 = jnp.where(rank1 == float(a), cnt[a:a + 1], c1)
    return c1, z, okc


def _route_head(s1, s2, k, exact):
    key_pos = lax.broadcasted_iota(jnp.int32, s1.shape, 0).astype(F32)
    v1, rank1, ok1 = _extract_topk(s1, k, key_pos, exact)
    v2, rank2, ok2 = _extract_topk(s2, k, key_pos, exact)
    c1, z, _ = _choose_pairs(v1, v2, rank1, k, True)
    ok = jnp.logical_and(ok1, ok2)
    return c1, jnp.exp(s1 - v1[0]), rank2, jnp.exp(s2 - v2[0]) / z, ok


def _peer_route_kernel(x_ref, wq_ref, keys_ref, c1_ref, e1_ref, r2_ref, e2_ref, wqbf_ref, *, heads, topk):
    n_keys, half = keys_ref.shape[1], keys_ref.shape[2]
    _cast_weights_once(pl.program_id(0), [(wq_ref, wqbf_ref)])
    q = _dot(x_ref[...], wqbf_ref[...])
    nt = (((1,), (1,)), ((), ()))

    def route_all(exact):
        ok = None
        for h in range(heads):
            base = h * 2 * half
            s1 = lax.dot_general(keys_ref[0], q[:, base:base + half], nt, preferred_element_type=F32)
            s2 = lax.dot_general(keys_ref[1], q[:, base + half:base + 2 * half], nt, preferred_element_type=F32)
            c1, e1, rank2, e2, ok_h = _route_head(s1, s2, topk, exact)
            c1_ref[h] = _dup_bf16_words(c1)
            e1_ref[h] = _dup_bf16_words(e1)
            r2_ref[h] = rank2.astype(BF16)
            e2_ref[h] = e2.astype(BF16)
            ok = ok_h if ok is None else jnp.logical_and(ok, ok_h)
        return ok

    ok = route_all(exact=False)
    n_bad = jnp.sum(jnp.where(ok, 0.0, 1.0))

    @pl.when(n_bad > 0.0)
    def _():
        route_all(exact=True)


def _dup_bf16_words(v):
    b = lax.bitcast_convert_type(v.astype(BF16).astype(F32), jnp.uint32)
    return b | (b >> 16)


def _peer_route(x_bf, wq, keys, heads, tt):
    t, d = x_bf.shape
    n_keys = keys.shape[1]
    out_w = jax.ShapeDtypeStruct((heads, n_keys, t), jnp.uint32)
    out_b = jax.ShapeDtypeStruct((heads, n_keys, t), BF16)
    ospec = pl.BlockSpec((heads, n_keys, tt), lambda i: (0, 0, i))
    return pl.pallas_call(
        functools.partial(_peer_route_kernel, heads=heads, topk=PEER_TOPK),
        grid=(t // tt,),
        in_specs=[
            pl.BlockSpec((tt, d), lambda i: (i, 0)),
            pl.BlockSpec(wq.shape, lambda i: (0, 0)),
            pl.BlockSpec(keys.shape, lambda i: (0, 0, 0)),
        ],
        out_specs=[ospec, ospec, ospec, ospec],
        out_shape=[out_w, out_w, out_b, out_b],
        scratch_shapes=[pltpu.VMEM(wq.shape, BF16)],
        compiler_params=_cparams("arbitrary"),
        name="peer_route",
    )(x_bf, wq, keys)


def _peer_dense_kernel(xn_ref, un_ref, vt_ref, c1_ref, e1_ref, r2_ref, e2_ref, o_ref, acc_ref, ht_ref, act_ref, *,
                       heads, n_e):
    s_idx = pl.program_id(0)
    e_gate = jnp.maximum(s_idx - 1, 0) % n_e
    e_down = jnp.maximum(s_idx - 2, 0) % n_e
    ec = un_ref.shape[0]
    tt = xn_ref.shape[0]
    n_keys = r2_ref.shape[1]
    rows_per_step = ec // n_keys

    @pl.when(s_idx == 0)
    def _():
        ht_ref[...] = jnp.zeros_like(ht_ref)
        act_ref[...] = jnp.zeros_like(act_ref)

    @pl.when(e_down == 0)
    def _():
        acc_ref[...] = jnp.zeros_like(acc_ref)

    acc_ref[...] += _dot(vt_ref[...], act_ref[...])
    ht = ht_ref[...]
    ht_ref[...] = lax.dot_general(un_ref[...], xn_ref[...], (((1,), (1,)), ((), ())), preferred_element_type=F32)

    def row_bf16(ref, h, i):
        words = jnp.broadcast_to(ref[h, pl.ds(i, 1), :], (n_keys // 2, tt))
        return pltpu.bitcast(words, BF16)

    for ii in range(rows_per_step):
        i = e_gate * rows_per_step + ii
        g = None
        for h in range(heads):
            sel = jnp.where(r2_ref[h] < row_bf16(c1_ref, h, i), e2_ref[h], jnp.zeros((), BF16))
            term = sel * row_bf16(e1_ref, h, i)
            g = term if g is None else g + term
        rows = slice(ii * n_keys, (ii + 1) * n_keys)
        act_ref[rows, :] = jax.nn.gelu(ht[rows, :]).astype(BF16) * g

    @pl.when(jnp.logical_and(e_down == n_e - 1, s_idx > 1))
    def _():
        o_ref[...] = acc_ref[...].T


def _peer_dense(x_bf, u_bf, vt_bf, c1, e1, r2, e2, tt, ec):
    t, d = x_bf.shape
    n_exp = u_bf.shape[0]
    heads, n_keys, _ = c1.shape
    n_t, n_e = t // tt, n_exp // ec
    n_chunks = n_t * n_e
    up = lambda s: jnp.minimum(s, n_chunks - 1)
    gate = lambda s: jnp.clip(s - 1, 0, n_chunks - 1)
    down = lambda s: jnp.maximum(s - 2, 0)
    rspec = pl.BlockSpec((heads, n_keys, tt), lambda s: (0, 0, gate(s) // n_e))
    return pl.pallas_call(
        functools.partial(_peer_dense_kernel, heads=heads, n_e=n_e),
        grid=(n_chunks + 2,),
        in_specs=[
            pl.BlockSpec((tt, d), lambda s: (up(s) // n_e, 0)),
            pl.BlockSpec((ec, d), lambda s: (up(s) % n_e, 0)),
            pl.BlockSpec((None, d, ec), lambda s: (down(s) % n_e, 0, 0)),
            rspec, rspec, rspec, rspec,
        ],
        out_specs=pl.BlockSpec((tt, d), lambda s: (down(s) // n_e, 0)),
        out_shape=jax.ShapeDtypeStruct((t, d), F32),
        scratch_shapes=[pltpu.VMEM((d, tt), F32), pltpu.VMEM((ec, tt), F32), pltpu.VMEM((ec, tt), BF16)],
        compiler_params=_cparams("arbitrary"),
        name="peer_dense",
    )(x_bf, u_bf, vt_bf, c1, e1, r2, e2)


def _transpose_cast_kernel(v_ref, o_ref):
    o_ref[...] = v_ref[...].T.astype(o_ref.dtype)


def _transpose_cast(v, te):
    n_exp, d = v.shape
    return pl.pallas_call(
        _transpose_cast_kernel,
        grid=(n_exp // te,),
        in_specs=[pl.BlockSpec((te, d), lambda e: (e, 0))],
        out_specs=pl.BlockSpec((None, d, te), lambda e: (e, 0, 0)),
        out_shape=jax.ShapeDtypeStruct((n_exp // te, d, te), BF16),
        compiler_params=_cparams("parallel"),
        name="transpose_cast",
    )(v)


def _tile(n, pref):
    t = min(n, pref)
    assert n % t == 0
    return t


def _layer(x, mem, w_in, b_in, rnn_conv_w, rnn_conv_b, rnn_w_a, rnn_b_a, rnn_w_x, rnn_b_x, rnn_lambda,
           w_rnn_out, conf_dw_w, conf_dw_b, conf_ln_g, conf_ln_b, w_conf_out, b_conf_out, w_mix_out,
           ln1_g, ln1_b, xa_w_q, xa_w_k, xa_w_v, xa_w_o, ln2_g, ln2_b,
           peer_w_q, peer_sub_keys, peer_u, peer_v, ln3_g, ln3_b, alpha):
    bsz, seq, d = x.shape
    t = bsz * seq
    mem_len = mem.shape[1]
    d_rnn = rnn_conv_w.shape[1]
    d_conv = conf_dw_w.shape[1]
    row = lambda v: v.reshape(1, -1)

    tm = _tile(t, 1024)
    tn = _tile(d, 1024)
    x2d = x.reshape(t, d)
    x_bf = x2d.astype(BF16)
    b_in2 = row(b_in)
    tn2 = _tile(d, 512)

    xr = _proj(x_bf, w_in, b_in2, 0, d_rnn, "none", F32, tm, tn)
    gg = _proj(x_bf, w_in, b_in2, d_rnn, d_rnn, "gelu", BF16, tm, tn)
    n_exp = peer_u.shape[0]
    ec = _tile(n_exp, 1024)
    v_rows = n_exp // ((d_conv // tn2) * (t // tm))
    if n_exp % ((d_conv // tn2) * (t // tm)) == 0 and v_rows % LANES == 0 and ec % v_rows == 0:
        cglu, vt_bf = _glu(x_bf, w_in, b_in2, 2 * d_rnn, 2 * d_rnn + d_conv, d_conv, tm, tn2,
                           side=peer_v, side_chunk=ec)
    else:
        cglu = _glu(x_bf, w_in, b_in2, 2 * d_rnn, 2 * d_rnn + d_conv, d_conv, tm, tn2)
        vt_bf = _transpose_cast(peer_v, ec)
    u_steps = (2 * d // tn) * (t // tm)
    if n_exp % u_steps == 0 and (n_exp // u_steps) % (2 * SUBLANES) == 0:
        mgates, u_bf = _proj(x_bf, w_in, b_in2, 2 * d_rnn + 2 * d_conv, 2 * d, "sigmoid", BF16, tm, tn, side=peer_u)
    else:
        mgates = _proj(x_bf, w_in, b_in2, 2 * d_rnn + 2 * d_conv, 2 * d, "sigmoid", BF16, tm, tn)
        u_bf = peer_u.astype(BF16)

    hr = _rnn_branch(xr.reshape(bsz, seq, d_rnn), rnn_conv_w, row(rnn_conv_b),
                     rnn_w_a.astype(BF16), rnn_w_x.astype(BF16), row(rnn_b_a), row(rnn_b_x),
                     row(rnn_lambda), gg.reshape(bsz, seq, d_rnn),
                     _tile(seq, 512), _tile(d_rnn, 512))
    cact = _conf_branch(cglu.reshape(bsz, seq, d_conv), conf_dw_w, row(conf_dw_b),
                        row(conf_ln_g), row(conf_ln_b), _tile(seq, 256))

    merged = _merge(hr.reshape(t, d_rnn), cact.reshape(t, d_conv), w_rnn_out,
                    w_conf_out, row(b_conf_out), mgates, d, tm, tn2)
    tl = _tile(t, 512)
    x1, x1_bf = _mm_ln(merged, w_mix_out, x2d, row(ln1_g), row(ln1_b), alpha, tl)

    zero_d = jnp.zeros((1, d), F32)
    q = _proj(x1_bf, xa_w_q, zero_d, 0, d, "none", BF16, tm, tn)
    mem_bf = mem.reshape(bsz * mem_len, d).astype(BF16)
    tmem = _tile(bsz * mem_len, 1024)
    k = _proj(mem_bf, xa_w_k, zero_d, 0, d, "none", BF16, tmem, tn)
    v = _proj(mem_bf, xa_w_v, zero_d, 0, d, "none", BF16, tmem, tn)
    o = _attention(q, k, v, bsz, seq, mem_len, d, _tile(seq, 512))
    x2, x2_bf = _mm_ln(o, xa_w_o, x1, row(ln2_g), row(ln2_b), alpha, tl)

    heads = peer_w_q.shape[1] // (2 * peer_sub_keys.shape[2])
    c1, e1, r2, e2 = _peer_route(x2_bf, peer_w_q, peer_sub_keys, heads, _tile(t, 256))
    ff = _peer_dense(x2_bf, u_bf, vt_bf, c1, e1, r2, e2, _tile(t, 512), ec)
    x3 = _add_ln(ff, x2, row(ln3_g), row(ln3_b), alpha, tl)
    return x3.reshape(bsz, seq, d)


def kernel(x, mem, w_in, b_in, rnn_conv_w, rnn_conv_b, rnn_w_a, rnn_b_a, rnn_w_x, rnn_b_x, rnn_lambda, w_rnn_out, conf_dw_w, conf_dw_b, conf_ln_g, conf_ln_b, w_conf_out, b_conf_out, w_mix_out, ln1_g, ln1_b, xa_w_q, xa_w_k, xa_w_v, xa_w_o, ln2_g, ln2_b, peer_w_q, peer_sub_keys, peer_u, peer_v, ln3_g, ln3_b):
    depth = w_in.shape[0]
    alpha = (2 * depth) ** 0.25
    params = (w_in, b_in, rnn_conv_w, rnn_conv_b, rnn_w_a, rnn_b_a, rnn_w_x, rnn_b_x, rnn_lambda, w_rnn_out,
              conf_dw_w, conf_dw_b, conf_ln_g, conf_ln_b, w_conf_out, b_conf_out, w_mix_out, ln1_g, ln1_b,
              xa_w_q, xa_w_k, xa_w_v, xa_w_o, ln2_g, ln2_b, peer_w_q, peer_sub_keys, peer_u, peer_v,
              ln3_g, ln3_b)
    for l in range(depth):
        x = _layer(x, mem, *[p[l] for p in params], alpha)
    return x
```

```python
import functools
import math

import jax
import jax.numpy as jnp
from jax import lax
from jax.experimental import pallas as pl
from jax.experimental.pallas import tpu as pltpu

F32 = jnp.float32
BF16 = jnp.bfloat16

LRU_C = 8.0
LN_EPS = 1e-5
XA_HEADS = 4
PEER_TOPK = 16
VMEM_LIMIT_BYTES = 56 * 1024 * 1024
LANES = 128
SUBLANES = 8


def _cparams(*sem):
    return pltpu.CompilerParams(dimension_semantics=sem, vmem_limit_bytes=VMEM_LIMIT_BYTES)


def _layer_norm_rows(z, g, b):
    mu = jnp.mean(z, axis=-1, keepdims=True)
    zc = z - mu
    var = jnp.mean(zc * zc, axis=-1, keepdims=True)
    return zc * lax.rsqrt(var + LN_EPS) * g + b


def _dot(a, b):
    return jnp.dot(a, b, preferred_element_type=F32)


def _cast_weights_once(step, pairs):
    @pl.when(step == 0)
    def _():
        for w_ref, wbf_ref in pairs:
            wbf_ref[...] = w_ref[...].astype(BF16)


def _proj_kernel(a_ref, w_ref, b_ref, o_ref, wbf_ref, *, act):
    _cast_weights_once(pl.program_id(1), [(w_ref, wbf_ref)])
    y = _dot(a_ref[...], wbf_ref[...]) + b_ref[...]
    if act == "gelu":
        y = jax.nn.gelu(y)
    elif act == "sigmoid":
        y = jax.nn.sigmoid(y)
    o_ref[...] = y.astype(o_ref.dtype)


def _proj_cast_kernel(a_ref, w_ref, b_ref, side_ref, o_ref, side_o_ref, wbf_ref, *, act):
    side_o_ref[...] = side_ref[...].astype(BF16)
    _proj_kernel(a_ref, w_ref, b_ref, o_ref, wbf_ref, act=act)


def _proj(a, w, b, col0, ncols, act, out_dtype, tm, tn, side=None):
    m, k = a.shape
    c0 = col0 // tn
    n_m = m // tm
    in_specs = [
        pl.BlockSpec((tm, k), lambda j, i: (i, 0)),
        pl.BlockSpec((k, tn), lambda j, i: (0, j + c0)),
        pl.BlockSpec((1, tn), lambda j, i: (0, j + c0)),
    ]
    out_specs = pl.BlockSpec((tm, tn), lambda j, i: (i, j))
    out_shape = jax.ShapeDtypeStruct((m, ncols), out_dtype)
    kern, args = functools.partial(_proj_kernel, act=act), (a, w, b)
    if side is not None:
        rs = side.shape[0] // ((ncols // tn) * n_m)
        side_spec = pl.BlockSpec((rs, side.shape[1]), lambda j, i: (j * n_m + i, 0))
        in_specs.append(side_spec)
        out_specs = [out_specs, side_spec]
        out_shape = [out_shape, jax.ShapeDtypeStruct(side.shape, BF16)]
        kern, args = functools.partial(_proj_cast_kernel, act=act), (a, w, b, side)
    return pl.pallas_call(
        kern,
        grid=(ncols // tn, n_m),
        in_specs=in_specs,
        out_specs=out_specs,
        out_shape=out_shape,
        scratch_shapes=[pltpu.VMEM((k, tn), BF16)],
        compiler_params=_cparams("parallel", "arbitrary"),
        name="proj_" + act,
    )(*args)


def _glu_kernel(a_ref, w1_ref, w2_ref, b1_ref, b2_ref, o_ref, w1bf_ref, w2bf_ref):
    _cast_weights_once(pl.program_id(1), [(w1_ref, w1bf_ref), (w2_ref, w2bf_ref)])
    a = a_ref[...]
    y1 = _dot(a, w1bf_ref[...]) + b1_ref[...]
    y2 = _dot(a, w2bf_ref[...]) + b2_ref[...]
    o_ref[...] = (y1 * jax.nn.sigmoid(y2)).astype(o_ref.dtype)


def _glu_transpose_kernel(a_ref, w1_ref, w2_ref, b1_ref, b2_ref, side_ref, o_ref, side_o_ref, w1bf_ref, w2bf_ref):
    side_o_ref[...] = side_ref[...].T.astype(BF16)
    _glu_kernel(a_ref, w1_ref, w2_ref, b1_ref, b2_ref, o_ref, w1bf_ref, w2bf_ref)


def _glu(a, w, b, col1, col2, ncols, tm, tn, side=None, side_chunk=None):
    m, k = a.shape
    c1, c2 = col1 // tn, col2 // tn
    n_m = m // tm
    in_specs = [
        pl.BlockSpec((tm, k), lambda j, i: (i, 0)),
        pl.BlockSpec((k, tn), lambda j, i: (0, j + c1)),
        pl.BlockSpec((k, tn), lambda j, i: (0, j + c2)),
        pl.BlockSpec((1, tn), lambda j, i: (0, j + c1)),
        pl.BlockSpec((1, tn), lambda j, i: (0, j + c2)),
    ]
    out_specs = pl.BlockSpec((tm, tn), lambda j, i: (i, j))
    out_shape = jax.ShapeDtypeStruct((m, ncols), F32)
    kern, args = _glu_kernel, (a, w, w, b, b)
    if side is not None:
        rows, cols = side.shape
        rs = rows // ((ncols // tn) * n_m)
        per_chunk = side_chunk // rs
        in_specs.append(pl.BlockSpec((rs, cols), lambda j, i: (j * n_m + i, 0)))
        out_specs = [out_specs, pl.BlockSpec((None, cols, rs), lambda j, i: ((j * n_m + i) // per_chunk, 0,
                                                                             (j * n_m + i) % per_chunk))]
        out_shape = [out_shape, jax.ShapeDtypeStruct((rows // side_chunk, cols, side_chunk), BF16)]
        kern, args = _glu_transpose_kernel, (a, w, w, b, b, side)
    return pl.pallas_call(
        kern,
        grid=(ncols // tn, n_m),
        in_specs=in_specs,
        out_specs=out_specs,
        out_shape=out_shape,
        scratch_shapes=[pltpu.VMEM((k, tn), BF16), pltpu.VMEM((k, tn), BF16)],
        compiler_params=_cparams("parallel", "arbitrary"),
        name="proj_glu",
    )(*args)


def _rnn_kernel(xr_ref, cw_ref, cb_ref, wa_ref, wx_ref, ba_ref, bx_ref, lam_ref, gg_ref, o_ref,
                ext_ref, a_ref, u_ref, h_ref):
    ts, ct = xr_ref.shape
    kw = cw_ref.shape[0]
    s_idx = pl.program_id(2)

    @pl.when(s_idx == 0)
    def _():
        ext_ref[0:SUBLANES, :] = jnp.zeros((SUBLANES, ct), F32)
        h_ref[...] = jnp.zeros_like(h_ref)

    ext_ref[SUBLANES:SUBLANES + ts, :] = xr_ref[...]
    y = jnp.zeros((ts, ct), F32) + cb_ref[...]
    ext = ext_ref[...]
    for k in range(kw):
        off = SUBLANES - (kw - 1) + k
        z = ext if off % SUBLANES == 0 else pltpu.roll(ext, ts + SUBLANES - off % SUBLANES, 0)
        base = off - off % SUBLANES
        y = y + cw_ref[k:k + 1, :] * z[base:base + ts, :]
    ext_ref[0:SUBLANES, :] = ext_ref[ts:ts + SUBLANES, :]

    nblk = ct // LANES
    r_parts, i_parts = [], []
    for blk in range(nblk):
        yb = y[:, blk * LANES:(blk + 1) * LANES].astype(BF16)
        r_parts.append(_dot(yb, wa_ref[blk]))
        i_parts.append(_dot(yb, wx_ref[blk]))
    r = jax.nn.sigmoid(jnp.concatenate(r_parts, axis=1) + ba_ref[...])
    ig = jax.nn.sigmoid(jnp.concatenate(i_parts, axis=1) + bx_ref[...])
    lam = lam_ref[...]
    sp = jnp.maximum(-lam, 0.0) + jnp.log1p(jnp.exp(-jnp.abs(lam)))
    log_a = (-LRU_C * r) * sp
    a = jnp.exp(log_a)
    u = jnp.sqrt(1.0 - jnp.exp(2.0 * log_a)) * (ig * y)
    a_ref[...] = a
    u_ref[...] = u

    row = lax.broadcasted_iota(jnp.int32, (SUBLANES, ct), 0)

    def body(g, h):
        r0 = pl.multiple_of(g * SUBLANES, SUBLANES)
        av = a_ref[pl.ds(r0, SUBLANES), :]
        uv = u_ref[pl.ds(r0, SUBLANES), :]
        for d in (1, 2, 4):
            a_sh = jnp.where(row >= d, pltpu.roll(av, d, 0), 1.0)
            u_sh = jnp.where(row >= d, pltpu.roll(uv, d, 0), 0.0)
            uv = av * u_sh + uv
            av = av * a_sh
        hh = uv + av * h
        u_ref[pl.ds(r0, SUBLANES), :] = hh
        return hh[SUBLANES - 1:SUBLANES, :]

    h_last = lax.fori_loop(0, ts // SUBLANES, body, h_ref[...], unroll=4)
    h_ref[...] = h_last
    o_ref[...] = (u_ref[...] * gg_ref[...].astype(F32)).astype(o_ref.dtype)


def _rnn_branch(xr, cw, cb, wa, wx, ba, bx, lam, gg, ts, ct):
    bsz, seq, c = xr.shape
    kw = cw.shape[0]
    assert kw - 1 <= SUBLANES
    nb = ct // LANES
    vec = pl.BlockSpec((1, ct), lambda b, j, s: (0, j))
    return pl.pallas_call(
        _rnn_kernel,
        grid=(bsz, c // ct, seq // ts),
        in_specs=[
            pl.BlockSpec((None, ts, ct), lambda b, j, s: (b, s, j)),
            pl.BlockSpec((kw, ct), lambda b, j, s: (0, j)),
            vec,
            pl.BlockSpec((nb, LANES, LANES), lambda b, j, s: (j, 0, 0)),
            pl.BlockSpec((nb, LANES, LANES), lambda b, j, s: (j, 0, 0)),
            vec, vec, vec,
            pl.BlockSpec((None, ts, ct), lambda b, j, s: (b, s, j)),
        ],
        out_specs=pl.BlockSpec((None, ts, ct), lambda b, j, s: (b, s, j)),
        out_shape=jax.ShapeDtypeStruct((bsz, seq, c), BF16),
        scratch_shapes=[
            pltpu.VMEM((ts + SUBLANES, ct), F32),
            pltpu.VMEM((ts, ct), F32),
            pltpu.VMEM((ts, ct), F32),
            pltpu.VMEM((1, ct), F32),
        ],
        compiler_params=_cparams("parallel", "parallel", "arbitrary"),
        name="rnn_branch",
    )(xr, cw, cb, wa, wx, ba, bx, lam, gg)


CONV_HALO = 32
CONV_LANE_CHUNK = 128
CONV_ROW_BLOCK = 128


def _conf_kernel(c_ref, dw_ref, db_ref, g_ref, b_ref, o_ref, ext_ref, y_ref):
    ts, ch = c_ref.shape
    kw = dw_ref.shape[0]
    s_idx = pl.program_id(1)

    @pl.when(s_idx == 0)
    def _():
        ext_ref[0:CONV_HALO, :] = jnp.zeros((CONV_HALO, ch), F32)

    ext_ref[CONV_HALO:CONV_HALO + ts, :] = c_ref[...]
    off0 = CONV_HALO - (kw - 1)
    rb_rows = min(CONV_ROW_BLOCK, ts)
    win_rows = CONV_HALO + rb_rows
    lc = min(CONV_LANE_CHUNK, ch)
    n_lane = ch // lc

    def body(it, carry):
        rb = it // n_lane
        lb = it % n_lane
        r0 = pl.multiple_of(rb * rb_rows, rb_rows)
        l0 = pl.multiple_of(lb * lc, lc)
        win = ext_ref[pl.ds(r0, win_rows), pl.ds(l0, lc)]
        acc = jnp.zeros((rb_rows, lc), F32) + db_ref[:, pl.ds(l0, lc)]
        for r in range(SUBLANES):
            z = win if r == 0 else pltpu.roll(win, win_rows - r, 0)
            for q in range(win_rows // SUBLANES):
                k = q * SUBLANES + r - off0
                if 0 <= k < kw:
                    acc = acc + dw_ref[k:k + 1, pl.ds(l0, lc)] * z[q * SUBLANES:q * SUBLANES + rb_rows, :]
        y_ref[pl.ds(r0, rb_rows), pl.ds(l0, lc)] = acc
        return carry

    lax.fori_loop(0, (ts // rb_rows) * n_lane, body, 0)
    ext_ref[0:CONV_HALO, :] = ext_ref[ts:ts + CONV_HALO, :]
    yn = _layer_norm_rows(y_ref[...], g_ref[...], b_ref[...])
    o_ref[...] = (yn * jax.nn.sigmoid(yn)).astype(o_ref.dtype)


def _conf_branch(c, dw, db, g, b, ts):
    bsz, seq, ch = c.shape
    kw = dw.shape[0]
    assert kw - 1 <= CONV_HALO
    vec = pl.BlockSpec((1, ch), lambda bi, s: (0, 0))
    return pl.pallas_call(
        _conf_kernel,
        grid=(bsz, seq // ts),
        in_specs=[
            pl.BlockSpec((None, ts, ch), lambda bi, s: (bi, s, 0)),
            pl.BlockSpec((kw, ch), lambda bi, s: (0, 0)),
            vec, vec, vec,
        ],
        out_specs=pl.BlockSpec((None, ts, ch), lambda bi, s: (bi, s, 0)),
        out_shape=jax.ShapeDtypeStruct((bsz, seq, ch), BF16),
        scratch_shapes=[
            pltpu.VMEM((ts + CONV_HALO, ch), F32),
            pltpu.VMEM((ts, ch), F32),
        ],
        compiler_params=_cparams("parallel", "arbitrary"),
        name="conf_branch",
    )(c, dw, db, g, b)


def _merge_kernel(hr_ref, c_ref, wr_ref, wc_ref, bc_ref, gr_ref, gc_ref, o_ref, wrbf_ref, wcbf_ref):
    _cast_weights_once(pl.program_id(1), [(wr_ref, wrbf_ref), (wc_ref, wcbf_ref)])
    yr = _dot(hr_ref[...], wrbf_ref[...])
    yc = _dot(c_ref[...], wcbf_ref[...]) + bc_ref[...]
    o_ref[...] = (gr_ref[...].astype(F32) * yr + gc_ref[...].astype(F32) * yc).astype(o_ref.dtype)


def _merge(hr, cact, wr, wc, bc, gates, d_model, tm, tn):
    m, k = hr.shape
    goff = d_model // tn
    return pl.pallas_call(
        _merge_kernel,
        grid=(d_model // tn, m // tm),
        in_specs=[
            pl.BlockSpec((tm, k), lambda j, i: (i, 0)),
            pl.BlockSpec((tm, k), lambda j, i: (i, 0)),
            pl.BlockSpec((k, tn), lambda j, i: (0, j)),
            pl.BlockSpec((k, tn), lambda j, i: (0, j)),
            pl.BlockSpec((1, tn), lambda j, i: (0, j)),
            pl.BlockSpec((tm, tn), lambda j, i: (i, j)),
            pl.BlockSpec((tm, tn), lambda j, i: (i, j + goff)),
        ],
        out_specs=pl.BlockSpec((tm, tn), lambda j, i: (i, j)),
        out_shape=jax.ShapeDtypeStruct((m, d_model), BF16),
        scratch_shapes=[pltpu.VMEM((k, tn), BF16), pltpu.VMEM((k, tn), BF16)],
        compiler_params=_cparams("parallel", "arbitrary"),
        name="merge",
    )(hr, cact, wr, wc, bc, gates, gates)


def _mm_ln_kernel(a_ref, w_ref, res_ref, g_ref, b_ref, o_ref, obf_ref, wbf_ref, *, alpha):
    _cast_weights_once(pl.program_id(0), [(w_ref, wbf_ref)])
    y = _dot(a_ref[...], wbf_ref[...])
    z = _layer_norm_rows(alpha * res_ref[...] + y, g_ref[...], b_ref[...])
    o_ref[...] = z
    obf_ref[...] = z.astype(BF16)


def _mm_ln(a, w, res, g, b, alpha, tm):
    m, k = a.shape
    n = w.shape[1]
    vec = pl.BlockSpec((1, n), lambda i: (0, 0))
    return pl.pallas_call(
        functools.partial(_mm_ln_kernel, alpha=alpha),
        grid=(m // tm,),
        in_specs=[
            pl.BlockSpec((tm, k), lambda i: (i, 0)),
            pl.BlockSpec((k, n), lambda i: (0, 0), pipeline_mode=pl.Buffered(1)),
            pl.BlockSpec((tm, n), lambda i: (i, 0)),
            vec, vec,
        ],
        out_specs=[pl.BlockSpec((tm, n), lambda i: (i, 0)), pl.BlockSpec((tm, n), lambda i: (i, 0))],
        out_shape=[jax.ShapeDtypeStruct((m, n), F32), jax.ShapeDtypeStruct((m, n), BF16)],
        scratch_shapes=[pltpu.VMEM((k, n), BF16)],
        compiler_params=_cparams("arbitrary"),
        name="mm_ln",
    )(a, w, res, g, b)


def _add_ln_kernel(y_ref, res_ref, g_ref, b_ref, o_ref, *, alpha):
    o_ref[...] = _layer_norm_rows(alpha * res_ref[...] + y_ref[...], g_ref[...], b_ref[...])


def _add_ln(y, res, g, b, alpha, tm):
    m, n = y.shape
    vec = pl.BlockSpec((1, n), lambda i: (0, 0))
    return pl.pallas_call(
        functools.partial(_add_ln_kernel, alpha=alpha),
        grid=(m // tm,),
        in_specs=[pl.BlockSpec((tm, n), lambda i: (i, 0)), pl.BlockSpec((tm, n), lambda i: (i, 0)), vec, vec],
        out_specs=pl.BlockSpec((tm, n), lambda i: (i, 0)),
        out_shape=jax.ShapeDtypeStruct((m, n), F32),
        compiler_params=_cparams("parallel"),
        name="add_ln",
    )(y, res, g, b)


def _attn_kernel(q_ref, k_ref, v_ref, o_ref, *, heads):
    tq, d = q_ref.shape
    hd = d // heads
    scale = hd ** -0.5
    outs = []
    for h in range(heads):
        sl = slice(h * hd, (h + 1) * hd)
        s = lax.dot_general(q_ref[:, sl], k_ref[:, sl], (((1,), (1,)), ((), ())),
                            preferred_element_type=F32) * scale
        m = jnp.max(s, axis=-1, keepdims=True)
        p = jnp.exp(s - m)
        p = p / jnp.sum(p, axis=-1, keepdims=True)
        outs.append(_dot(p.astype(BF16), v_ref[:, sl]))
    o_ref[...] = jnp.concatenate(outs, axis=1).astype(o_ref.dtype)


def _attention(q, k, v, bsz, seq, mem_len, d_model, tq):
    nq = seq // tq
    return pl.pallas_call(
        functools.partial(_attn_kernel, heads=XA_HEADS),
        grid=(bsz, nq),
        in_specs=[
            pl.BlockSpec((tq, d_model), lambda b, i: (b * nq + i, 0)),
            pl.BlockSpec((mem_len, d_model), lambda b, i: (b, 0)),
            pl.BlockSpec((mem_len, d_model), lambda b, i: (b, 0)),
        ],
        out_specs=pl.BlockSpec((tq, d_model), lambda b, i: (b * nq + i, 0)),
        out_shape=jax.ShapeDtypeStruct((bsz * seq, d_model), BF16),
        compiler_params=_cparams("parallel", "parallel"),
        name="xattn",
    )(q, k, v)


def _extract_topk(s, k, pos, exact):
    rank = jnp.full(s.shape, float(k), F32)
    vals = []
    for r in range(k):
        m = jnp.max(s, axis=0, keepdims=True)
        hit = s == m
        if exact:
            first = jnp.min(jnp.where(hit, pos, jnp.inf), axis=0, keepdims=True)
            hit = pos == first
        rank = jnp.where(hit, float(r), rank)
        s = jnp.where(hit, -jnp.inf, s)
        vals.append(m)
    n_ranked = jnp.sum(jnp.where(rank < float(k), 1.0, 0.0), axis=0, keepdims=True)
    return vals, rank, n_ranked == float(k)


def _candidate_blocks(k):
    blocks = [("row", 0, k), ("row", 1, k // 2)]
    b = 0
    while 3 * (b + 1) <= k:
        hi = k // (b + 1)
        blocks.append(("col", b, -(-hi // SUBLANES) * SUBLANES, 2, hi))
        b += 1
    return blocks


def _choose_pairs(v1, v2, rank1, k, exact):
    tt = rank1.shape[1]
    v1_all = jnp.concatenate(v1, axis=0)
    v2_all = jnp.concatenate(v2, axis=0)
    e1_all = jnp.exp(v1_all - v1[0])
    e2_all = jnp.exp(v2_all - v2[0])
    cands, prods, poss = [], [], []
    for blk in _candidate_blocks(k):
        if blk[0] == "row":
            _, a, n = blk
            cands.append(v1[a] + v2_all[0:n])
            prods.append(e1_all[a:a + 1] * e2_all[0:n])
            poss.append((a * k + lax.broadcasted_iota(jnp.int32, (n, tt), 0)).astype(F32))
        else:
            _, b, n, lo, hi = blk
            a_col = lax.broadcasted_iota(jnp.int32, (n, tt), 0)
            valid = jnp.logical_and(a_col >= lo, a_col < hi)
            cands.append(jnp.where(valid, v1_all[0:n] + v2[b], -jnp.inf))
            prods.append(e1_all[0:n] * e2_all[b:b + 1])
            poss.append(jnp.where(valid, (a_col * k + b).astype(F32), jnp.inf))
    _, rankc, okc = _extract_topk(jnp.concatenate(cands, axis=0), k, jnp.concatenate(poss, axis=0), exact)
    sel = jnp.where(rankc < float(k), 1.0, 0.0)
    z = jnp.sum(sel * jnp.concatenate(prods, axis=0), axis=0, keepdims=True)
    cnt = jnp.zeros((k, tt), F32)
    a_col = lax.broadcasted_iota(jnp.int32, (k, tt), 0)
    row0 = 0
    for blk in _candidate_blocks(k):
        n = blk[2]
        part = sel[row0:row0 + n]
        row0 += n
        if blk[0] == "row":
            cnt = cnt + jnp.where(a_col == blk[1], jnp.sum(part, axis=0, keepdims=True), 0.0)
        elif n == k:
            cnt = cnt + part
        else:
            cnt = cnt + jnp.concatenate([part, jnp.zeros((k - n, part.shape[1]), F32)], axis=0)
    c1 = jnp.zeros_like(rank1)
    for a in range(k):
        c1 = jnp.where(rank1 == float(a), cnt[a:a + 1], c1)
    return c1, z, okc


def _route_head(s1, s2, k, exact):
    key_pos = lax.broadcasted_iota(jnp.int32, s1.shape, 0).astype(F32)
    v1, rank1, ok1 = _extract_topk(s1, k, key_pos, exact)
    v2, rank2, ok2 = _extract_topk(s2, k, key_pos, exact)
    c1, z, _ = _choose_pairs(v1, v2, rank1, k, True)
    ok = jnp.logical_and(ok1, ok2)
    return c1, jnp.exp(s1 - v1[0]), rank2, jnp.exp(s2 - v2[0]) / z, ok


def _peer_route_kernel(x_ref, wq_ref, keys_ref, c1_ref, e1_ref, r2_ref, e2_ref, wqbf_ref, *, heads, topk):
    n_keys, half = keys_ref.shape[1], keys_ref.shape[2]
    _cast_weights_once(pl.program_id(0), [(wq_ref, wqbf_ref)])
    q = _dot(x_ref[...], wqbf_ref[...])
    nt = (((1,), (1,)), ((), ()))

    def route_one(h, exact):
        base = h * 2 * half
        s1 = lax.dot_general(keys_ref[0], q[:, base:base + half], nt, preferred_element_type=F32)
        s2 = lax.dot_general(keys_ref[1], q[:, base + half:base + 2 * half], nt, preferred_element_type=F32)
        c1, e1, rank2, e2, ok = _route_head(s1, s2, topk, exact)
        c1_ref[h] = _dup_bf16_words(c1)
        e1_ref[h] = _dup_bf16_words(e1)
        r2_ref[h] = rank2.astype(BF16)
        e2_ref[h] = e2.astype(BF16)
        return jnp.sum(jnp.where(ok, 0.0, 1.0))

    n_bad = [route_one(h, False) for h in range(heads)]
    total_bad = functools.reduce(lambda a, b: a + b, n_bad)

    @pl.when(total_bad > 0.0)
    def _():
        for h in range(heads):
            @pl.when(n_bad[h] > 0.0)
            def _(h=h):
                route_one(h, True)


def _dup_bf16_words(v):
    b = lax.bitcast_convert_type(v.astype(BF16).astype(F32), jnp.uint32)
    return b | (b >> 16)


def _peer_route(x_bf, wq, keys, heads, tt):
    t, d = x_bf.shape
    n_keys = keys.shape[1]
    out_w = jax.ShapeDtypeStruct((heads, n_keys, t), jnp.uint32)
    out_b = jax.ShapeDtypeStruct((heads, n_keys, t), BF16)
    ospec = pl.BlockSpec((heads, n_keys, tt), lambda i: (0, 0, i))
    return pl.pallas_call(
        functools.partial(_peer_route_kernel, heads=heads, topk=PEER_TOPK),
        grid=(t // tt,),
        in_specs=[
            pl.BlockSpec((tt, d), lambda i: (i, 0)),
            pl.BlockSpec(wq.shape, lambda i: (0, 0)),
            pl.BlockSpec(keys.shape, lambda i: (0, 0, 0)),
        ],
        out_specs=[ospec, ospec, ospec, ospec],
        out_shape=[out_w, out_w, out_b, out_b],
        scratch_shapes=[pltpu.VMEM(wq.shape, BF16)],
        compiler_params=_cparams("arbitrary"),
        name="peer_route",
    )(x_bf, wq, keys)


def _peer_dense_kernel(xn_ref, un_ref, vt_ref, c1_ref, e1_ref, r2_ref, e2_ref, o_ref, acc_ref, ht_ref, act_ref, *,
                       heads, n_e):
    s_idx = pl.program_id(0)
    e_gate = jnp.maximum(s_idx - 1, 0) % n_e
    e_down = jnp.maximum(s_idx - 2, 0) % n_e
    ec = un_ref.shape[0]
    tt = xn_ref.shape[0]
    n_keys = r2_ref.shape[1]
    rows_per_step = ec // n_keys

    @pl.when(s_idx == 0)
    def _():
        ht_ref[...] = jnp.zeros_like(ht_ref)
        act_ref[...] = jnp.zeros_like(act_ref)

    @pl.when(e_down == 0)
    def _():
        acc_ref[...] = jnp.zeros_like(acc_ref)

    acc_ref[...] += _dot(vt_ref[...], act_ref[...])
    ht = ht_ref[...]
    ht_ref[...] = lax.dot_general(un_ref[...], xn_ref[...], (((1,), (1,)), ((), ())), preferred_element_type=F32)

    def row_bf16(ref, h, i):
        words = jnp.broadcast_to(ref[h, pl.ds(i, 1), :], (n_keys // 2, tt))
        return pltpu.bitcast(words, BF16)

    for ii in range(rows_per_step):
        i = e_gate * rows_per_step + ii
        g = None
        for h in range(heads):
            sel = jnp.where(r2_ref[h] < row_bf16(c1_ref, h, i), e2_ref[h], jnp.zeros((), BF16))
            term = sel * row_bf16(e1_ref, h, i)
            g = term if g is None else g + term
        rows = slice(ii * n_keys, (ii + 1) * n_keys)
        act_ref[rows, :] = jax.nn.gelu(ht[rows, :]).astype(BF16) * g

    @pl.when(jnp.logical_and(e_down == n_e - 1, s_idx > 1))
    def _():
        o_ref[...] = acc_ref[...].T


def _peer_dense(x_bf, u_bf, vt_bf, c1, e1, r2, e2, tt, ec):
    t, d = x_bf.shape
    n_exp = u_bf.shape[0]
    heads, n_keys, _ = c1.shape
    n_t, n_e = t // tt, n_exp // ec
    n_chunks = n_t * n_e
    up = lambda s: jnp.minimum(s, n_chunks - 1)
    gate = lambda s: jnp.clip(s - 1, 0, n_chunks - 1)
    down = lambda s: jnp.maximum(s - 2, 0)
    rspec = pl.BlockSpec((heads, n_keys, tt), lambda s: (0, 0, gate(s) // n_e))
    return pl.pallas_call(
        functools.partial(_peer_dense_kernel, heads=heads, n_e=n_e),
        grid=(n_chunks + 2,),
        in_specs=[
            pl.BlockSpec((tt, d), lambda s: (up(s) // n_e, 0)),
            pl.BlockSpec((ec, d), lambda s: (up(s) % n_e, 0)),
            pl.BlockSpec((None, d, ec), lambda s: (down(s) % n_e, 0, 0)),
            rspec, rspec, rspec, rspec,
        ],
        out_specs=pl.BlockSpec((tt, d), lambda s: (down(s) // n_e, 0)),
        out_shape=jax.ShapeDtypeStruct((t, d), F32),
        scratch_shapes=[pltpu.VMEM((d, tt), F32), pltpu.VMEM((ec, tt), F32), pltpu.VMEM((ec, tt), BF16)],
        compiler_params=_cparams("arbitrary"),
        name="peer_dense",
    )(x_bf, u_bf, vt_bf, c1, e1, r2, e2)


def _transpose_cast_kernel(v_ref, o_ref):
    o_ref[...] = v_ref[...].T.astype(o_ref.dtype)


def _transpose_cast(v, te):
    n_exp, d = v.shape
    return pl.pallas_call(
        _transpose_cast_kernel,
        grid=(n_exp // te,),
        in_specs=[pl.BlockSpec((te, d), lambda e: (e, 0))],
        out_specs=pl.BlockSpec((None, d, te), lambda e: (e, 0, 0)),
        out_shape=jax.ShapeDtypeStruct((n_exp // te, d, te), BF16),
        compiler_params=_cparams("parallel"),
        name="transpose_cast",
    )(v)


def _tile(n, pref):
    t = min(n, pref)
    assert n % t == 0
    return t


def _layer(x, mem, w_in, b_in, rnn_conv_w, rnn_conv_b, rnn_w_a, rnn_b_a, rnn_w_x, rnn_b_x, rnn_lambda,
           w_rnn_out, conf_dw_w, conf_dw_b, conf_ln_g, conf_ln_b, w_conf_out, b_conf_out, w_mix_out,
           ln1_g, ln1_b, xa_w_q, xa_w_k, xa_w_v, xa_w_o, ln2_g, ln2_b,
           peer_w_q, peer_sub_keys, peer_u, peer_v, ln3_g, ln3_b, alpha):
    bsz, seq, d = x.shape
    t = bsz * seq
    mem_len = mem.shape[1]
    d_rnn = rnn_conv_w.shape[1]
    d_conv = conf_dw_w.shape[1]
    row = lambda v: v.reshape(1, -1)

    tm = _tile(t, 1024)
    tn = _tile(d, 1024)
    x2d = x.reshape(t, d)
    x_bf = x2d.astype(BF16)
    b_in2 = row(b_in)
    tn2 = _tile(d, 512)

    xr = _proj(x_bf, w_in, b_in2, 0, d_rnn, "none", F32, tm, tn)
    gg = _proj(x_bf, w_in, b_in2, d_rnn, d_rnn, "gelu", BF16, tm, tn)
    n_exp = peer_u.shape[0]
    ec = _tile(n_exp, 1024)
    v_rows = n_exp // ((d_conv // tn2) * (t // tm))
    if n_exp % ((d_conv // tn2) * (t // tm)) == 0 and v_rows % LANES == 0 and ec % v_rows == 0:
        cglu, vt_bf = _glu(x_bf, w_in, b_in2, 2 * d_rnn, 2 * d_rnn + d_conv, d_conv, tm, tn2,
                           side=peer_v, side_chunk=ec)
    else:
        cglu = _glu(x_bf, w_in, b_in2, 2 * d_rnn, 2 * d_rnn + d_conv, d_conv, tm, tn2)
        vt_bf = _transpose_cast(peer_v, ec)
    u_steps = (2 * d // tn) * (t // tm)
    if n_exp % u_steps == 0 and (n_exp // u_steps) % (2 * SUBLANES) == 0:
        mgates, u_bf = _proj(x_bf, w_in, b_in2, 2 * d_rnn + 2 * d_conv, 2 * d, "sigmoid", BF16, tm, tn, side=peer_u)
    else:
        mgates = _proj(x_bf, w_in, b_in2, 2 * d_rnn + 2 * d_conv, 2 * d, "sigmoid", BF16, tm, tn)
        u_bf = peer_u.astype(BF16)

    hr = _rnn_branch(xr.reshape(bsz, seq, d_rnn), rnn_conv_w, row(rnn_conv_b),
                     rnn_w_a.astype(BF16), rnn_w_x.astype(BF16), row(rnn_b_a), row(rnn_b_x),
                     row(rnn_lambda), gg.reshape(bsz, seq, d_rnn),
                     _tile(seq, 512), _tile(d_rnn, 512))
    cact = _conf_branch(cglu.reshape(bsz, seq, d_conv), conf_dw_w, row(conf_dw_b),
                        row(conf_ln_g), row(conf_ln_b), _tile(seq, 256))

    merged = _merge(hr.reshape(t, d_rnn), cact.reshape(t, d_conv), w_rnn_out,
                    w_conf_out, row(b_conf_out), mgates, d, tm, tn2)
    tl = _tile(t, 512)
    x1, x1_bf = _mm_ln(merged, w_mix_out, x2d, row(ln1_g), row(ln1_b), alpha, tl)

    zero_d = jnp.zeros((1, d), F32)
    q = _proj(x1_bf, xa_w_q, zero_d, 0, d, "none", BF16, tm, tn)
    mem_bf = mem.reshape(bsz * mem_len, d).astype(BF16)
    tmem = _tile(bsz * mem_len, 1024)
    k = _proj(mem_bf, xa_w_k, zero_d, 0, d, "none", BF16, tmem, tn)
    v = _proj(mem_bf, xa_w_v, zero_d, 0, d, "none", BF16, tmem, tn)
    o = _attention(q, k, v, bsz, seq, mem_len, d, _tile(seq, 512))
    x2, x2_bf = _mm_ln(o, xa_w_o, x1, row(ln2_g), row(ln2_b), alpha, tl)

    heads = peer_w_q.shape[1] // (2 * peer_sub_keys.shape[2])
    c1, e1, r2, e2 = _peer_route(x2_bf, peer_w_q, peer_sub_keys, heads, _tile(t, 256))
    ff = _peer_dense(x2_bf, u_bf, vt_bf, c1, e1, r2, e2, _tile(t, 512), ec)
    x3 = _add_ln(ff, x2, row(ln3_g), row(ln3_b), alpha, tl)
    return x3.reshape(bsz, seq, d)


def kernel(x, mem, w_in, b_in, rnn_conv_w, rnn_conv_b, rnn_w_a, rnn_b_a, rnn_w_x, rnn_b_x, rnn_lambda, w_rnn_out, conf_dw_w, conf_dw_b, conf_ln_g, conf_ln_b, w_conf_out, b_conf_out, w_mix_out, ln1_g, ln1_b, xa_w_q, xa_w_k, xa_w_v, xa_w_o, ln2_g, ln2_b, peer_w_q, peer_sub_keys, peer_u, peer_v, ln3_g, ln3_b):
    depth = w_in.shape[0]
    alpha = (2 * depth) ** 0.25
    params = (w_in, b_in, rnn_conv_w, rnn_conv_b, rnn_w_a, rnn_b_a, rnn_w_x, rnn_b_x, rnn_lambda, w_rnn_out,
              conf_dw_w, conf_dw_b, conf_ln_g, conf_ln_b, w_conf_out, b_conf_out, w_mix_out, ln1_g, ln1_b,
              xa_w_q, xa_w_k, xa_w_v, xa_w_o, ln2_g, ln2_b, peer_w_q, peer_sub_keys, peer_u, peer_v,
              ln3_g, ln3_b)
    for l in range(depth):
        x = _layer(x, mem, *[p[l] for p in params], alpha)
    return x
```
